```python
import jax, jax.numpy as jnp
from jax import lax
import numpy as np

D_MODEL = 1024
BATCH = 4
SEQ = 4096
DEPTH = 2

N_A_LAYERS = DEPTH // 2
N_B_LAYERS = DEPTH - N_A_LAYERS

D_FF = 2816
RMS_EPS = 1e-6

RWKV_HEAD = 64
RWKV_HEADS = D_MODEL // RWKV_HEAD
DECAY_LORA = 64
AAA_LORA = 64
GATE_LORA = 128
GN_EPS = 64e-5

ATT_HEAD = 64
Q_HEADS = D_MODEL // ATT_HEAD
KV_HEADS = 4
GROUP = Q_HEADS // KV_HEADS
WINDOW = 128
BLOCK = 128
KV_WIDTH = KV_HEADS * ATT_HEAD
MASK_VALUE = -1e30

kernel_name = "yoco_rwkv7_swa_sink_macaron"


def rms_norm(x, g):
    xf = x.astype(jnp.float32)
    y = xf * lax.rsqrt(jnp.mean(xf * xf, axis=-1, keepdims=True) + RMS_EPS)
    return (y * g.astype(jnp.float32)).astype(x.dtype)


def swiglu(x, w_in, w_out):
    gate, up = jnp.split(x @ w_in, 2, axis=-1)
    return (jax.nn.silu(gate) * up) @ w_out


def rwkv7_time_mix(x, mu, w_rkv, w_o, w0, w1, w2, a0, a1, a2, g1, g2,
                   k_k, k_a, r_k, gn_g, gn_b):
    B, T, C = x.shape
    H, N = RWKV_HEADS, RWKV_HEAD
    f32 = jnp.float32
    x_prev = jnp.pad(x, ((0, 0), (1, 0), (0, 0)))[:, :-1]
    xx = x_prev - x
    xr, xw, xk, xv, xa, xg = [x + xx * mu[i] for i in range(6)]

    r = xr @ w_rkv[0]
    k = xk @ w_rkv[1]
    v = xv @ w_rkv[2]
    w = -jax.nn.softplus(-(w0 + jnp.tanh(xw @ w1) @ w2)) - 0.5
    decay = jnp.exp(-jnp.exp(w.astype(f32)))
    a = jax.nn.sigmoid(a0 + (xa @ a1) @ a2)
    g = jax.nn.sigmoid(xg @ g1) @ g2

    heads = lambda t: t.reshape(B, T, H, N).astype(f32)
    kk = heads(k * k_k)
    kk = kk / jnp.maximum(jnp.linalg.norm(kk, axis=-1, keepdims=True), 1e-12)
    k = k * (1.0 + (a - 1.0) * k_a)
    r_h, k_h, v_h, a_h, w_h = heads(r), heads(k), heads(v), heads(a), heads(decay)
    b_h = kk * a_h

    def step(S, inp):
        r_t, w_t, k_t, v_t, kk_t, b_t = inp
        sa = jnp.einsum('bhij,bhj->bhi', S, -kk_t)
        S = (S * w_t[:, :, None, :] + sa[..., None] * b_t[:, :, None, :]
             + v_t[..., None] * k_t[:, :, None, :])
        y = jnp.einsum('bhij,bhj->bhi', S, r_t)
        return S, y

    xs = tuple(jnp.moveaxis(t, 1, 0) for t in (r_h, w_h, k_h, v_h, kk, b_h))
    S0 = jnp.zeros((B, H, N, N), f32)
    _, y = lax.scan(step, S0, xs)
    y = jnp.moveaxis(y, 0, 1)

    mean = jnp.mean(y, axis=-1, keepdims=True)
    var = jnp.mean(jnp.square(y - mean), axis=-1, keepdims=True)
    y = ((y - mean) * lax.rsqrt(var + GN_EPS)).reshape(B, T, C)
    y = y * gn_g.astype(f32) + gn_b.astype(f32)
    bonus = jnp.sum(r_h * k_h * r_k.astype(f32), axis=-1, keepdims=True) * v_h
    y = (y + bonus.reshape(B, T, C)).astype(x.dtype)
    return (y * g) @ w_o


def swa_sinks(xq, k_sh, v_sh, w_q, b_q, w_o, b_o, sinks):
    B, T, _ = xq.shape
    NB = T // BLOCK
    q = (xq @ w_q + b_q).reshape(B, NB, BLOCK, KV_HEADS, GROUP, ATT_HEAD)

    def banded(t):
        t = t.reshape(B, NB, BLOCK, KV_HEADS, ATT_HEAD)
        prev = jnp.pad(t, ((0, 0), (1, 0), (0, 0), (0, 0), (0, 0)))[:, :-1]
        return jnp.concatenate([prev, t], axis=2)

    kb, vb = banded(k_sh), banded(v_sh)
    s = jnp.einsum('bnqkgd,bnskd->bnkgqs', q, kb).astype(jnp.float32) * (ATT_HEAD ** -0.5)
    blk = jnp.arange(NB)[:, None, None] * BLOCK
    q_pos = blk + jnp.arange(BLOCK)[None, :, None]
    k_pos = blk - BLOCK + jnp.arange(2 * BLOCK)[None, None, :]
    valid = (k_pos >= 0) & (k_pos <= q_pos) & (q_pos - k_pos < WINDOW)
    s = jnp.where(valid[None, :, None, None], s, MASK_VALUE)
    sink = jnp.broadcast_to(
        sinks.astype(jnp.float32).reshape(1, 1, KV_HEADS, GROUP, 1, 1), s.shape[:-1] + (1,))
    p = jax.nn.softmax(jnp.concatenate([s, sink], axis=-1), axis=-1)[..., :-1]
    o = jnp.einsum('bnkgqs,bnskd->bnqkgd', p.astype(vb.dtype), vb)
    o = o.reshape(B, T, Q_HEADS * ATT_HEAD)
    return o @ w_o + b_o


def setup_inputs(seed: int = 0) -> dict:
    key = jax.random.key(seed)
    ks = iter(jax.random.split(key, 32))
    D, NA, NBL = D_MODEL, N_A_LAYERS, N_B_LAYERS
    f32 = jnp.float32

    def nrm(shape, scale):
        return scale * jax.random.normal(next(ks), shape, f32)

    def uni(shape, lo, hi):
        return jax.random.uniform(next(ks), shape, f32, lo, hi)

    return {
        "x": nrm((BATCH, SEQ, D), 1.0),
        "norm_g": 1.0 + nrm((DEPTH, 6, D), 0.05),
        "ffn_w_in": nrm((DEPTH, 2, D, 2 * D_FF), D ** -0.5),
        "ffn_w_out": nrm((DEPTH, 2, D_FF, D), D_FF ** -0.5),
        "rwkv_mu": uni((NA, 6, D), 0.0, 1.0),
        "rwkv_w_rkv": nrm((NA, 3, D, D), D ** -0.5),
        "rwkv_w_o": nrm((NA, D, D), D ** -0.5),
        "rwkv_w0": uni((NA, D), -6.0, -1.0),
        "rwkv_w1": nrm((NA, D, DECAY_LORA), D ** -0.5),
        "rwkv_w2": nrm((NA, DECAY_LORA, D), 0.1 * DECAY_LORA ** -0.5),
        "rwkv_a0": nrm((NA, D), 0.1),
        "rwkv_a1": nrm((NA, D, AAA_LORA), D ** -0.5),
        "rwkv_a2": nrm((NA, AAA_LORA, D), 0.1 * AAA_LORA ** -0.5),
        "rwkv_g1": nrm((NA, D, GATE_LORA), D ** -0.5),
        "rwkv_g2": nrm((NA, GATE_LORA, D), GATE_LORA ** -0.5),
        "rwkv_k_k": 0.85 + nrm((NA, D), 0.05),
        "rwkv_k_a": 1.0 + nrm((NA, D), 0.05),
        "rwkv_r_k": nrm((NA, RWKV_HEADS, RWKV_HEAD), 0.1),
        "rwkv_gn_g": 1.0 + nrm((NA, D), 0.05),
        "rwkv_gn_b": nrm((NA, D), 0.02),
        "kv_norm_g": 1.0 + nrm((D,), 0.05),
        "w_kv": nrm((D, 2 * KV_WIDTH), D ** -0.5),
        "b_kv": nrm((2 * KV_WIDTH,), 0.02),
        "attn_w_q": nrm((NBL, D, Q_HEADS * ATT_HEAD), D ** -0.5),
        "attn_b_q": nrm((NBL, Q_HEADS * ATT_HEAD), 0.02),
        "attn_w_o": nrm((NBL, Q_HEADS * ATT_HEAD, D), (Q_HEADS * ATT_HEAD) ** -0.5),
        "attn_b_o": nrm((NBL, D), 0.02),
        "attn_sinks": nrm((NBL, Q_HEADS), 1.0),
    }


def reference(x, norm_g, ffn_w_in, ffn_w_out,
              rwkv_mu, rwkv_w_rkv, rwkv_w_o, rwkv_w0, rwkv_w1, rwkv_w2,
              rwkv_a0, rwkv_a1, rwkv_a2, rwkv_g1, rwkv_g2, rwkv_k_k, rwkv_k_a,
              rwkv_r_k, rwkv_gn_g, rwkv_gn_b,
              kv_norm_g, w_kv, b_kv,
              attn_w_q, attn_b_q, attn_w_o, attn_b_o, attn_sinks):
    B, T, _ = x.shape
    h = x
    k_sh = v_sh = None
    for layer in range(DEPTH):
        g = norm_g[layer]
        h = h + 0.5 * rms_norm(swiglu(rms_norm(h, g[0]), ffn_w_in[layer, 0], ffn_w_out[layer, 0]), g[1])
        u = rms_norm(h, g[2])
        if layer < N_A_LAYERS:
            i = layer
            m = rwkv7_time_mix(u, rwkv_mu[i], rwkv_w_rkv[i], rwkv_w_o[i], rwkv_w0[i],
                               rwkv_w1[i], rwkv_w2[i], rwkv_a0[i], rwkv_a1[i], rwkv_a2[i],
                               rwkv_g1[i], rwkv_g2[i], rwkv_k_k[i], rwkv_k_a[i],
                               rwkv_r_k[i], rwkv_gn_g[i], rwkv_gn_b[i])
        else:
            j = layer - N_A_LAYERS
            m = swa_sinks(u, k_sh, v_sh, attn_w_q[j], attn_b_q[j], attn_w_o[j],
                          attn_b_o[j], attn_sinks[j])
        h = h + rms_norm(m, g[3])
        h = h + 0.5 * rms_norm(swiglu(rms_norm(h, g[4]), ffn_w_in[layer, 1], ffn_w_out[layer, 1]), g[5])
        if layer == N_A_LAYERS - 1:
            kv = rms_norm(h, kv_norm_g) @ w_kv + b_kv
            k_sh = kv[..., :KV_WIDTH].reshape(B, T, KV_HEADS, ATT_HEAD)
            v_sh = kv[..., KV_WIDTH:].reshape(B, T, KV_HEADS, ATT_HEAD)
    return h
```

```python
import jax
import jax.numpy as jnp
from jax import lax
from jax.experimental import pallas as pl
from jax.experimental.pallas import tpu as pltpu

F32 = jnp.float32
BF16 = jnp.bfloat16

RMS_EPS = 1e-6
GN_EPS = 64e-5
HEAD = 64
WINDOW = 128
MASK_VALUE = -1e30
KV_HEADS = 4
GROUP = 4

CHUNK = 64
GROUP_LANES = 256
HEADS_PER_GROUP = GROUP_LANES // HEAD
BASE_BLOCK = 8
SEG_PAD = 128

VMEM_LIMIT = 56 * 1024 * 1024


def _rms(x, g):
    return x * lax.rsqrt(jnp.mean(x * x, axis=-1, keepdims=True) + RMS_EPS) * g


def _mm(a, b):
    return jnp.dot(a.astype(BF16), b.astype(BF16), preferred_element_type=F32)


def _mm_nt(a, b):
    return lax.dot_general(a.astype(BF16), b.astype(BF16), (((1,), (1,)), ((), ())),
                           preferred_element_type=F32)


def _split2(x):
    hi = x.astype(BF16)
    lo = (x - hi.astype(F32)).astype(BF16)
    return hi, lo


def _split3(x):
    h1 = x.astype(BF16)
    r1 = x - h1.astype(F32)
    h2 = r1.astype(BF16)
    h3 = (r1 - h2.astype(F32)).astype(BF16)
    return h1, h2, h3


def _seg_sum(x, e, et):
    hi, lo = _split2(x)
    s = jnp.dot(hi, e, preferred_element_type=F32) + jnp.dot(lo, e, preferred_element_type=F32)
    shi, slo = _split2(s)
    return jnp.dot(shi, et, preferred_element_type=F32) + jnp.dot(slo, et, preferred_element_type=F32)


def _full_spec(x):
    return pl.BlockSpec(x.shape, lambda *_: (0,) * x.ndim)


def _ffn_kernel(h_ref, gpre_ref, wg_ref, wu_ref, wo_ref, gpost_ref, o_ref, xn_ref, acc_ref):
    f = pl.program_id(1)
    nf = pl.num_programs(1)

    @pl.when(f == 0)
    def _():
        xn_ref[...] = _rms(h_ref[...], gpre_ref[...]).astype(BF16)

    xn = xn_ref[...]
    gate = jnp.dot(xn, wg_ref[...], preferred_element_type=F32)
    up = jnp.dot(xn, wu_ref[...], preferred_element_type=F32)
    act = (gate * jax.nn.sigmoid(gate) * up).astype(BF16)
    part = jnp.dot(act, wo_ref[...], preferred_element_type=F32)

    @pl.when(f == 0)
    def _():
        acc_ref[...] = part

    @pl.when(f > 0)
    def _():
        acc_ref[...] += part

    @pl.when(f == nf - 1)
    def _():
        o_ref[...] = h_ref[...] + 0.5 * _rms(acc_ref[...], gpost_ref[...])


def _ffn_block(h, g_pre, w_in, w_out, g_post, *, tm=512, tf=1408):
    m, d = h.shape
    d_ff = w_out.shape[0]
    nf = d_ff // tf
    return pl.pallas_call(
        _ffn_kernel,
        out_shape=jax.ShapeDtypeStruct((m, d), F32),
        grid=(m // tm, nf),
        in_specs=[
            pl.BlockSpec((tm, d), lambda i, j: (i, 0)),
            pl.BlockSpec((1, d), lambda i, j: (0, 0)),
            pl.BlockSpec((d, tf), lambda i, j: (0, j)),
            pl.BlockSpec((d, tf), lambda i, j: (0, nf + j)),
            pl.BlockSpec((tf, d), lambda i, j: (j, 0)),
            pl.BlockSpec((1, d), lambda i, j: (0, 0)),
        ],
        out_specs=pl.BlockSpec((tm, d), lambda i, j: (i, 0)),
        scratch_shapes=[pltpu.VMEM((tm, d), BF16), pltpu.VMEM((tm, d), F32)],
        compiler_params=pltpu.CompilerParams(
            dimension_semantics=("parallel", "arbitrary"), vmem_limit_bytes=VMEM_LIMIT),
        name="ffn_block",
    )(h, g_pre, w_in, w_in, w_out, g_post)


def _rwkv_prep_kernel(h_ref, g_ref, mu_ref, wr_ref, wk_ref, wv_ref, w0_ref, w1_ref, w2_ref,
                      a0_ref, a1_ref, a2_ref, g1_ref, g2_ref, kk_ref, ka_ref, rk_ref,
                      e_ref, et_ref,
                      r_out, k_out, v_out, lw_out, kk_out, a_out, g_out, bonus_out,
                      carry_ref):
    t = pl.program_id(1)
    u = _rms(h_ref[0], g_ref[...])
    tm = u.shape[0]

    @pl.when(t == 0)
    def _():
        carry_ref[...] = jnp.zeros_like(carry_ref)

    row = lax.broadcasted_iota(jnp.int32, u.shape, 0)
    prev = jnp.where(row == 0, carry_ref[0:1, :], pltpu.roll(u, shift=1, axis=0))
    carry_ref[0:1, :] = u[tm - 1:tm, :]
    xx = prev - u
    mu = mu_ref[...]
    xr, xw, xk, xv, xa, xg = [u + xx * mu[i:i + 1, :] for i in range(6)]

    r = _mm(xr, wr_ref[...])
    k = _mm(xk, wk_ref[...])
    v = _mm(xv, wv_ref[...])
    z = -(w0_ref[...] + _mm(jnp.tanh(_mm(xw, w1_ref[...])), w2_ref[...]))
    softplus = jnp.maximum(z, 0.0) + jnp.log1p(jnp.exp(-jnp.abs(z)))
    w = -softplus - 0.5
    a = jax.nn.sigmoid(a0_ref[...] + _mm(_mm(xa, a1_ref[...]), a2_ref[...]))
    g = _mm(jax.nn.sigmoid(_mm(xg, g1_ref[...])), g2_ref[...])

    e, et = e_ref[...], et_ref[...]
    kk = k * kk_ref[...]
    norm = jnp.sqrt(_seg_sum(kk * kk, e, et))
    kk = kk / jnp.maximum(norm, 1e-12)
    k = k * (1.0 + (a - 1.0) * ka_ref[...])

    r_out[0] = r
    k_out[0] = k
    v_out[0] = v
    lw_out[0] = -jnp.exp(w)
    kk_out[0] = kk
    a_out[0] = a
    g_out[0] = g
    bonus_out[0] = _seg_sum(r * k * rk_ref[...], e, et) * v


def _rwkv_prep(h3, params, *, tm=256):
    b, t, d = h3.shape
    tile = pl.BlockSpec((1, tm, d), lambda i, j: (i, j, 0))
    return pl.pallas_call(
        _rwkv_prep_kernel,
        out_shape=[jax.ShapeDtypeStruct((b, t, d), F32)] * 8,
        grid=(b, t // tm),
        in_specs=[tile] + [_full_spec(p) for p in params],
        out_specs=[tile] * 8,
        scratch_shapes=[pltpu.VMEM((8, d), F32)],
        compiler_params=pltpu.CompilerParams(
            dimension_semantics=("parallel", "arbitrary"), vmem_limit_bytes=VMEM_LIMIT),
        name="rwkv_prep",
    )(h3, *params)


def _scan_masks():
    c, n = CHUNK, GROUP_LANES
    row_s = lax.broadcasted_iota(jnp.int32, (c, n), 0)
    col_s = lax.broadcasted_iota(jnp.int32, (c, n), 1) % c
    strict = row_s > col_s
    incl = row_s >= col_s
    eye = row_s == col_s
    base = strict & ((row_s // BASE_BLOCK) == (col_s // BASE_BLOCK))
    offs = []
    b = BASE_BLOCK
    while b < c:
        offs.append(((row_s // (2 * b)) == (col_s // (2 * b)))
                    & ((row_s // b) % 2 == 1) & ((col_s // b) % 2 == 0))
        b *= 2
    row_b = lax.broadcasted_iota(jnp.int32, (n, n), 0)
    col_b = lax.broadcasted_iota(jnp.int32, (n, n), 1)
    mask_bd = (row_b // HEAD) == (col_b // HEAD)
    eye_bd = row_b == col_b
    tri_ones = jnp.where(lax.broadcasted_iota(jnp.int32, (c, c), 0)
                         >= lax.broadcasted_iota(jnp.int32, (c, c), 1), 1.0, 0.0).astype(BF16)
    return mask_bd, eye_bd, strict, incl, eye, base, tuple(offs), tri_ones


def _block_diag(z, mask_bd):
    tiled = jnp.concatenate([z] * HEADS_PER_GROUP, axis=0)
    return jnp.where(mask_bd, tiled, 0.0)


def _scan_group(r, k, v, lw, kk, a, s_bd, masks):
    mask_bd, eye_bd, strict, incl, eye, base, offs, tri_ones = masks
    c, n = CHUNK, GROUP_LANES
    bd = lambda z: _block_diag(z, mask_bd)

    l_cum = sum(jnp.dot(tri_ones, piece, preferred_element_type=F32) for piece in _split3(lw))
    e_l = jnp.exp(l_cum)
    e_nl = jnp.exp(-l_cum)
    a_t = -kk * jnp.exp(l_cum - lw)
    r_t = r * e_l
    b_t = kk * a * e_nl
    k_t = k * e_nl
    p_end = e_l[c - 1:c, :]
    b_h = b_t * p_end
    k_h = k_t * p_end

    a_all = _mm_nt(jnp.concatenate([a_t, r_t], axis=0),
                   jnp.concatenate([bd(b_t), bd(k_t)], axis=0))
    a_ab = a_all[:c, :n]
    a_ak = jnp.where(strict, a_all[:c, n:], 0.0)
    a_rb = jnp.where(incl, a_all[c:, :n], 0.0)
    a_rk = jnp.where(incl, a_all[c:, n:], 0.0)

    p = jnp.where(base, a_ab, 0.0)
    inv = jnp.where(eye, 1.0, 0.0) + p
    steps = BASE_BLOCK.bit_length() - 2
    for i in range(steps):
        p = _mm(p, bd(p))
        inv = inv + _mm(inv, bd(p))
    for off in offs:
        x = _mm(jnp.where(off, a_ab, 0.0), bd(inv))
        inv = inv + _mm(inv, bd(x))

    av = _mm(jnp.concatenate([a_ak, a_rk], axis=0), bd(v))
    a_ak_v, a_rk_v = av[:c], av[c:]
    wz = _mm(inv, jnp.concatenate([bd(a_t), bd(a_ak_v)], axis=1))
    w_m, z_m = wz[:, :n], wz[:, n:]
    qy = _mm(a_rb, jnp.concatenate([bd(w_m), bd(z_m)], axis=1))
    q_h = r_t + qy[:, :n]
    y_h = qy[:, n:] + a_rk_v

    lhs_t = jnp.concatenate([b_h, k_h], axis=0).T
    rhs = jnp.concatenate([jnp.concatenate([w_m, z_m], axis=1),
                           jnp.concatenate([jnp.zeros_like(v), v], axis=1)], axis=0)
    gh = _mm(lhs_t, rhs)
    g_bd = jnp.where(mask_bd, gh[:, :n], 0.0) + jnp.where(eye_bd, jnp.broadcast_to(p_end, (n, n)), 0.0)
    h_bd = jnp.where(mask_bd, gh[:, n:], 0.0)

    sy = _mm(jnp.concatenate([g_bd, q_h], axis=0), s_bd)
    return sy[n:] + y_h, sy[:n] + h_bd


def _rwkv_scan_kernel(r_ref, k_ref, v_ref, lw_ref, kk_ref, a_ref, y_ref, s_ref):
    @pl.when(pl.program_id(1) == 0)
    def _():
        s_ref[...] = jnp.zeros_like(s_ref)

    masks = _scan_masks()
    for gi in range(s_ref.shape[0]):
        sl = slice(gi * GROUP_LANES, (gi + 1) * GROUP_LANES)
        y, s_new = _scan_group(r_ref[0, :, sl], k_ref[0, :, sl], v_ref[0, :, sl], lw_ref[0, :, sl],
                               kk_ref[0, :, sl], a_ref[0, :, sl], s_ref[gi], masks)
        y_ref[0, :, sl] = y
        s_ref[gi] = s_new


def _rwkv_scan(r, k, v, lw, kk, a):
    b, t, d = r.shape
    tile = pl.BlockSpec((1, CHUNK, d), lambda i, j: (i, j, 0))
    return pl.pallas_call(
        _rwkv_scan_kernel,
        out_shape=jax.ShapeDtypeStruct((b, t, d), F32),
        grid=(b, t // CHUNK),
        in_specs=[tile] * 6,
        out_specs=tile,
        scratch_shapes=[pltpu.VMEM((d // GROUP_LANES, GROUP_LANES, GROUP_LANES), F32)],
        compiler_params=pltpu.CompilerParams(
            dimension_semantics=("parallel", "arbitrary"), vmem_limit_bytes=VMEM_LIMIT),
        name="rwkv_scan",
    )(r, k, v, lw, kk, a)


def _rwkv_out_kernel(h_ref, y_ref, bonus_ref, g_ref, gng_ref, gnb_ref, wo_ref, gpost_ref,
                     e_ref, et_ref, o_ref):
    e, et = e_ref[...], et_ref[...]
    y = y_ref[...]
    mean = _seg_sum(y, e, et) * (1.0 / HEAD)
    d = y - mean
    var = _seg_sum(d * d, e, et) * (1.0 / HEAD)
    yn = d * lax.rsqrt(var + GN_EPS) * gng_ref[...] + gnb_ref[...] + bonus_ref[...]
    m = _mm(yn * g_ref[...], wo_ref[...])
    o_ref[...] = h_ref[...] + _rms(m, gpost_ref[...])


def _rwkv_out(h, y, bonus, g, gn_g, gn_b, w_o, g_post, e, et, *, tm=512):
    m, d = h.shape
    tile = pl.BlockSpec((tm, d), lambda i: (i, 0))
    params = (gn_g, gn_b, w_o, g_post, e, et)
    return pl.pallas_call(
        _rwkv_out_kernel,
        out_shape=jax.ShapeDtypeStruct((m, d), F32),
        grid=(m // tm,),
        in_specs=[tile] * 4 + [_full_spec(p) for p in params],
        out_specs=tile,
        compiler_params=pltpu.CompilerParams(
            dimension_semantics=("parallel",), vmem_limit_bytes=VMEM_LIMIT),
        name="rwkv_out",
    )(h, y, bonus, g, *params)


def _kv_proj_kernel(h_ref, g_ref, w_ref, b_ref, o_ref):
    o_ref[...] = _mm(_rms(h_ref[...], g_ref[...]), w_ref[...]) + b_ref[...]


def _kv_proj(h, g, w, bias, *, tm=512):
    m, d = h.shape
    n = w.shape[1]
    return pl.pallas_call(
        _kv_proj_kernel,
        out_shape=jax.ShapeDtypeStruct((m, n), F32),
        grid=(m // tm,),
        in_specs=[pl.BlockSpec((tm, d), lambda i: (i, 0)), _full_spec(g), _full_spec(w), _full_spec(bias)],
        out_specs=pl.BlockSpec((tm, n), lambda i: (i, 0)),
        compiler_params=pltpu.CompilerParams(
            dimension_semantics=("parallel",), vmem_limit_bytes=VMEM_LIMIT),
        name="kv_proj",
    )(h, g, w, bias)


def _attn_kernel(sinks_ref, h_ref, kp_ref, kc_ref, vp_ref, vc_ref, gpre_ref, wq_ref, bq_ref,
                 wo_ref, bo_ref, gpost_ref, o_ref):
    nb = pl.program_id(1)
    h = h_ref[0]
    q = _mm(_rms(h, gpre_ref[...]), wq_ref[...]) + bq_ref[...]
    k_band = jnp.concatenate([kp_ref[0], kc_ref[0]], axis=0)
    v_band = jnp.concatenate([vp_ref[0], vc_ref[0]], axis=0)

    qi = lax.broadcasted_iota(jnp.int32, (WINDOW, 2 * WINDOW), 0)
    si = lax.broadcasted_iota(jnp.int32, (WINDOW, 2 * WINDOW), 1)
    valid = (si > qi) & (si <= qi + WINDOW) & ((si >= WINDOW) | (nb > 0))

    outs = []
    for kh in range(KV_HEADS):
        k_h = k_band[:, kh * HEAD:(kh + 1) * HEAD]
        v_h = v_band[:, kh * HEAD:(kh + 1) * HEAD]
        for gi in range(GROUP):
            hd = kh * GROUP + gi
            s = _mm_nt(q[:, hd * HEAD:(hd + 1) * HEAD], k_h) * (HEAD ** -0.5)
            s = jnp.where(valid, s, MASK_VALUE)
            sink = sinks_ref[hd]
            mx = jnp.maximum(jnp.max(s, axis=-1, keepdims=True), sink)
            p = jnp.exp(s - mx)
            denom = jnp.sum(p, axis=-1, keepdims=True) + jnp.exp(sink - mx)
            outs.append(_mm(p / denom, v_h))
    o = jnp.concatenate(outs, axis=1)
    m = _mm(o, wo_ref[...]) + bo_ref[...]
    o_ref[0] = h + _rms(m, gpost_ref[...])


def _attn_block(h3, kv3, sinks, g_pre, w_q, b_q, w_o, b_o, g_post):
    b, t, d = h3.shape
    kvw = KV_HEADS * HEAD
    tile = pl.BlockSpec((1, WINDOW, d), lambda i, j: (i, j, 0))
    params = (g_pre, w_q, b_q, w_o, b_o, g_post)
    return pl.pallas_call(
        _attn_kernel,
        out_shape=jax.ShapeDtypeStruct((b, t, d), F32),
        grid=(b, t // WINDOW),
        in_specs=[
            pl.BlockSpec(memory_space=pltpu.SMEM),
            tile,
            pl.BlockSpec((1, WINDOW, kvw), lambda i, j: (i, jnp.maximum(j - 1, 0), 0)),
            pl.BlockSpec((1, WINDOW, kvw), lambda i, j: (i, j, 0)),
            pl.BlockSpec((1, WINDOW, kvw), lambda i, j: (i, jnp.maximum(j - 1, 0), 1)),
            pl.BlockSpec((1, WINDOW, kvw), lambda i, j: (i, j, 1)),
        ] + [_full_spec(p) for p in params],
        out_specs=tile,
        compiler_params=pltpu.CompilerParams(
            dimension_semantics=("parallel", "parallel"), vmem_limit_bytes=VMEM_LIMIT),
        name="swa_block",
    )(sinks, h3, kv3, kv3, kv3, kv3, *params)


def kernel(x, norm_g, ffn_w_in, ffn_w_out, rwkv_mu, rwkv_w_rkv, rwkv_w_o, rwkv_w0, rwkv_w1, rwkv_w2, rwkv_a0, rwkv_a1, rwkv_a2, rwkv_g1, rwkv_g2, rwkv_k_k, rwkv_k_a, rwkv_r_k, rwkv_gn_g, rwkv_gn_b, kv_norm_g, w_kv, b_kv, attn_w_q, attn_b_q, attn_w_o, attn_b_o, attn_sinks):
    b, t, d = x.shape
    m = b * t
    depth = norm_g.shape[0]
    n_a = rwkv_mu.shape[0]
    row = lambda vec: vec.reshape(1, -1).astype(F32)
    bf = lambda w: w.astype(BF16)

    seg = (jnp.arange(d)[:, None] // HEAD == jnp.arange(SEG_PAD)[None, :]).astype(BF16)
    seg_t = seg.T

    h = x.reshape(m, d)
    kv = None
    for layer in range(depth):
        g = norm_g[layer]
        h = _ffn_block(h, row(g[0]), bf(ffn_w_in[layer, 0]), bf(ffn_w_out[layer, 0]), row(g[1]))
        if layer < n_a:
            i = layer
            params = (row(g[2]), rwkv_mu[i], bf(rwkv_w_rkv[i, 0]), bf(rwkv_w_rkv[i, 1]),
                      bf(rwkv_w_rkv[i, 2]), row(rwkv_w0[i]), bf(rwkv_w1[i]), bf(rwkv_w2[i]),
                      row(rwkv_a0[i]), bf(rwkv_a1[i]), bf(rwkv_a2[i]), bf(rwkv_g1[i]), bf(rwkv_g2[i]),
                      row(rwkv_k_k[i]), row(rwkv_k_a[i]), row(rwkv_r_k[i]), seg, seg_t)
            r, k, v, lw, kk, a, gate, bonus = _rwkv_prep(h.reshape(b, t, d), params)
            y = _rwkv_scan(r, k, v, lw, kk, a)
            h = _rwkv_out(h, y.reshape(m, d), bonus.reshape(m, d), gate.reshape(m, d),
                          row(rwkv_gn_g[i]), row(rwkv_gn_b[i]), bf(rwkv_w_o[i]), row(g[3]), seg, seg_t)
        else:
            j = layer - n_a
            h = _attn_block(h.reshape(b, t, d), kv, attn_sinks[j].astype(F32), row(g[2]),
                            bf(attn_w_q[j]), row(attn_b_q[j]), bf(attn_w_o[j]), row(attn_b_o[j]),
                            row(g[3])).reshape(m, d)
        h = _ffn_block(h, row(g[4]), bf(ffn_w_in[layer, 1]), bf(ffn_w_out[layer, 1]), row(g[5]))
        if layer == n_a - 1:
            kv = _kv_proj(h, row(kv_norm_g), bf(w_kv), row(b_kv)).reshape(b, t, -1)
    return h.reshape(b, t, d)
```

```python
import jax
import jax.numpy as jnp
from jax import lax
from jax.experimental import pallas as pl
from jax.experimental.pallas import tpu as pltpu

F32 = jnp.float32
BF16 = jnp.bfloat16

RMS_EPS = 1e-6
GN_EPS = 64e-5
HEAD = 64
WINDOW = 128
MASK_VALUE = -1e30
KV_HEADS = 4
GROUP = 4

CHUNK = 64
GROUP_LANES = 256
HEADS_PER_GROUP = GROUP_LANES // HEAD
BASE_BLOCK = 8
SEG_PAD = 128

VMEM_LIMIT = 56 * 1024 * 1024
MXU_WIDTH = 256


def _rms(x, g):
    return x * lax.rsqrt(jnp.mean(x * x, axis=-1, keepdims=True) + RMS_EPS) * g


def _mm(a, b):
    return jnp.dot(a.astype(BF16), b.astype(BF16), preferred_element_type=F32)


def _mm_nt(a, b):
    return lax.dot_general(a.astype(BF16), b.astype(BF16), (((1,), (1,)), ((), ())),
                           preferred_element_type=F32)


def _split2(x):
    hi = x.astype(BF16)
    lo = (x - hi.astype(F32)).astype(BF16)
    return hi, lo


def _split3(x):
    h1 = x.astype(BF16)
    r1 = x - h1.astype(F32)
    h2 = r1.astype(BF16)
    h3 = (r1 - h2.astype(F32)).astype(BF16)
    return h1, h2, h3


def _seg_sum(x, e, et):
    s = jnp.dot(x.astype(BF16), e, preferred_element_type=F32)
    shi, slo = _split2(s)
    return jnp.dot(shi, et, preferred_element_type=F32) + jnp.dot(slo, et, preferred_element_type=F32)


def _full_spec(x):
    return pl.BlockSpec(x.shape, lambda *_: (0,) * x.ndim)


def _ffn_chunks(d_ff):
    tiles = d_ff // MXU_WIDTH
    first = (tiles + 1) // 2 * MXU_WIDTH
    return ((0, first), (first, d_ff))


def _ffn_kernel(h_ref, gpre_ref, win_ref, wo_ref, gpost_ref, o_ref):
    d_ff = wo_ref.shape[0]
    h = h_ref[...]
    xn = _rms(h, gpre_ref[...]).astype(BF16)
    acc = None
    for lo, hi in _ffn_chunks(d_ff):
        gate = jnp.dot(xn, win_ref[:, lo:hi], preferred_element_type=F32)
        up = jnp.dot(xn, win_ref[:, d_ff + lo:d_ff + hi], preferred_element_type=F32)
        act = (gate * jax.nn.sigmoid(gate) * up).astype(BF16)
        part = jnp.dot(act, wo_ref[lo:hi, :], preferred_element_type=F32)
        acc = part if acc is None else acc + part
    o_ref[...] = h + 0.5 * _rms(acc, gpost_ref[...])


def _ffn_block(h, g_pre, w_in, w_out, g_post, *, tm=512):
    m, d = h.shape
    resident = lambda x: pl.BlockSpec(x.shape, lambda i: (0,) * x.ndim, pipeline_mode=pl.Buffered(1))
    return pl.pallas_call(
        _ffn_kernel,
        out_shape=jax.ShapeDtypeStruct((m, d), F32),
        grid=(m // tm,),
        in_specs=[pl.BlockSpec((tm, d), lambda i: (i, 0)), resident(g_pre), resident(w_in),
                  resident(w_out), resident(g_post)],
        out_specs=pl.BlockSpec((tm, d), lambda i: (i, 0)),
        compiler_params=pltpu.CompilerParams(
            dimension_semantics=("parallel",), vmem_limit_bytes=VMEM_LIMIT),
        name="ffn_block",
    )(h, g_pre, w_in, w_out, g_post)


def _rwkv_prep_kernel(h_ref, g_ref, mu_ref, wr_ref, wk_ref, wv_ref, w0_ref, w1_ref, w2_ref,
                      a0_ref, a1_ref, a2_ref, g1_ref, g2_ref, kk_ref, ka_ref, rk_ref,
                      e_ref, et_ref,
                      r_out, k_out, v_out, lw_out, kk_out, a_out, g_out, bonus_out,
                      carry_ref):
    t = pl.program_id(1)
    u = _rms(h_ref[0], g_ref[...])
    tm = u.shape[0]

    @pl.when(t == 0)
    def _():
        carry_ref[...] = jnp.zeros_like(carry_ref)

    row = lax.broadcasted_iota(jnp.int32, u.shape, 0)
    prev = jnp.where(row == 0, carry_ref[0:1, :], pltpu.roll(u, shift=1, axis=0))
    carry_ref[0:1, :] = u[tm - 1:tm, :]
    xx = prev - u
    mu = mu_ref[...]
    xr, xw, xk, xv, xa, xg = [u + xx * mu[i:i + 1, :] for i in range(6)]

    r = _mm(xr, wr_ref[...])
    k = _mm(xk, wk_ref[...])
    v = _mm(xv, wv_ref[...])
    z = -(w0_ref[...] + _mm(jnp.tanh(_mm(xw, w1_ref[...])), w2_ref[...]))
    softplus = jnp.maximum(z, 0.0) + jnp.log1p(jnp.exp(-jnp.abs(z)))
    w = -softplus - 0.5
    a = jax.nn.sigmoid(a0_ref[...] + _mm(_mm(xa, a1_ref[...]), a2_ref[...]))
    g = _mm(jax.nn.sigmoid(_mm(xg, g1_ref[...])), g2_ref[...])

    e, et = e_ref[...], et_ref[...]
    kk = k * kk_ref[...]
    norm = jnp.sqrt(_seg_sum(kk * kk, e, et))
    kk = kk / jnp.maximum(norm, 1e-12)
    k = k * (1.0 + (a - 1.0) * ka_ref[...])

    r_out[0] = r
    k_out[0] = k
    v_out[0] = v
    lw_out[0] = -jnp.exp(w)
    kk_out[0] = kk
    a_out[0] = a
    g_out[0] = g
    bonus_out[0] = _seg_sum(r * k * rk_ref[...], e, et) * v


def _rwkv_prep(h3, params, *, tm=256):
    b, t, d = h3.shape
    tile = pl.BlockSpec((1, tm, d), lambda i, j: (i, j, 0))
    return pl.pallas_call(
        _rwkv_prep_kernel,
        out_shape=[jax.ShapeDtypeStruct((b, t, d), F32)] * 8,
        grid=(b, t // tm),
        in_specs=[tile] + [_full_spec(p) for p in params],
        out_specs=[tile] * 8,
        scratch_shapes=[pltpu.VMEM((8, d), F32)],
        compiler_params=pltpu.CompilerParams(
            dimension_semantics=("parallel", "arbitrary"), vmem_limit_bytes=VMEM_LIMIT),
        name="rwkv_prep",
    )(h3, *params)


def _scan_masks():
    c, n = CHUNK, GROUP_LANES
    row_s = lax.broadcasted_iota(jnp.int32, (c, n), 0)
    col_s = lax.broadcasted_iota(jnp.int32, (c, n), 1) % c
    strict = row_s > col_s
    incl = row_s >= col_s
    eye = row_s == col_s
    base = strict & ((row_s // BASE_BLOCK) == (col_s // BASE_BLOCK))
    offs = []
    b = BASE_BLOCK
    while b < c:
        offs.append(((row_s // (2 * b)) == (col_s // (2 * b)))
                    & ((row_s // b) % 2 == 1) & ((col_s // b) % 2 == 0))
        b *= 2
    row_b = lax.broadcasted_iota(jnp.int32, (n, n), 0)
    col_b = lax.broadcasted_iota(jnp.int32, (n, n), 1)
    mask_bd = (row_b // HEAD) == (col_b // HEAD)
    eye_bd = row_b == col_b
    tri_ones = jnp.where(lax.broadcasted_iota(jnp.int32, (c, c), 0)
                         >= lax.broadcasted_iota(jnp.int32, (c, c), 1), 1.0, 0.0).astype(BF16)
    return mask_bd, eye_bd, strict, incl, eye, base, tuple(offs), tri_ones


def _block_diag(z, mask_bd):
    tiled = jnp.concatenate([z] * HEADS_PER_GROUP, axis=0)
    return jnp.where(mask_bd, tiled, 0.0)


def _scan_groups(r, k, v, lw, kk, a, s_bd, masks):
    mask_bd, eye_bd, strict, incl, eye, base, offs, tri_ones = masks
    c, n = CHUNK, GROUP_LANES
    groups = range(len(r))
    bd = lambda z: _block_diag(z, mask_bd)
    cat0 = lambda *xs: jnp.concatenate(xs, axis=0)
    cat1 = lambda *xs: jnp.concatenate(xs, axis=1)

    l_cum = [sum(jnp.dot(tri_ones, piece, preferred_element_type=F32) for piece in _split3(lw[g]))
             for g in groups]
    e_l = [jnp.exp(l_cum[g]) for g in groups]
    e_nl = [jnp.exp(-l_cum[g]) for g in groups]
    a_t = [-kk[g] * jnp.exp(l_cum[g] - lw[g]) for g in groups]
    r_t = [r[g] * e_l[g] for g in groups]
    b_t = [kk[g] * a[g] * e_nl[g] for g in groups]
    k_t = [k[g] * e_nl[g] for g in groups]
    p_end = [e_l[g][c - 1:c, :] for g in groups]

    a_all = [_mm_nt(cat0(a_t[g], r_t[g]), cat0(bd(b_t[g]), bd(k_t[g]))) for g in groups]
    a_ab = [a_all[g][:c, :n] for g in groups]

    p = [jnp.where(base, a_ab[g], 0.0) for g in groups]
    inv = [jnp.where(eye, 1.0, 0.0) + p[g] for g in groups]
    p = [_mm(p[g], bd(p[g])) for g in groups]
    for _ in range(BASE_BLOCK.bit_length() - 3):
        both = [_mm(cat0(p[g], inv[g]), bd(p[g])) for g in groups]
        p = [both[g][:c] for g in groups]
        inv = [inv[g] + both[g][c:] for g in groups]
    inv = [inv[g] + _mm(inv[g], bd(p[g])) for g in groups]
    for off in offs:
        x = [_mm(jnp.where(off, a_ab[g], 0.0), bd(inv[g])) for g in groups]
        inv = [inv[g] + _mm(inv[g], bd(x[g])) for g in groups]

    av = [_mm(cat0(jnp.where(strict, a_all[g][:c, n:], 0.0), jnp.where(incl, a_all[g][c:, n:], 0.0)),
              bd(v[g])) for g in groups]
    wz = [_mm(inv[g], cat1(bd(a_t[g]), bd(av[g][:c]))) for g in groups]
    qy = [_mm(jnp.where(incl, a_all[g][c:, :n], 0.0), cat1(bd(wz[g][:, :n]), bd(wz[g][:, n:])))
          for g in groups]
    gh = [_mm(cat0(b_t[g] * p_end[g], k_t[g] * p_end[g]).T,
              cat0(wz[g], cat1(jnp.zeros_like(v[g]), v[g]))) for g in groups]
    g_bd = [jnp.where(mask_bd, gh[g][:, :n], 0.0)
            + jnp.where(eye_bd, jnp.broadcast_to(p_end[g], (n, n)), 0.0) for g in groups]
    sy = [_mm(cat0(g_bd[g], r_t[g] + qy[g][:, :n]), s_bd[g]) for g in groups]
    y = [sy[g][n:] + qy[g][:, n:] + av[g][c:] for g in groups]
    s_new = [sy[g][:n] + jnp.where(mask_bd, gh[g][:, n:], 0.0) for g in groups]
    return y, s_new


def _rwkv_scan_kernel(r_ref, k_ref, v_ref, lw_ref, kk_ref, a_ref, y_ref, s_ref):
    @pl.when(pl.program_id(1) == 0)
    def _():
        s_ref[...] = jnp.zeros_like(s_ref)

    ng = s_ref.shape[0]
    lanes = [slice(g * GROUP_LANES, (g + 1) * GROUP_LANES) for g in range(ng)]
    load = lambda ref: [ref[0, :, sl] for sl in lanes]
    y, s_new = _scan_groups(load(r_ref), load(k_ref), load(v_ref), load(lw_ref), load(kk_ref),
                            load(a_ref), [s_ref[g] for g in range(ng)], _scan_masks())
    for g in range(ng):
        y_ref[0, :, lanes[g]] = y[g]
        s_ref[g] = s_new[g]


def _rwkv_scan(r, k, v, lw, kk, a):
    b, t, d = r.shape
    tile = pl.BlockSpec((1, CHUNK, d), lambda i, j: (i, j, 0))
    return pl.pallas_call(
        _rwkv_scan_kernel,
        out_shape=jax.ShapeDtypeStruct((b, t, d), F32),
        grid=(b, t // CHUNK),
        in_specs=[tile] * 6,
        out_specs=tile,
        scratch_shapes=[pltpu.VMEM((d // GROUP_LANES, GROUP_LANES, GROUP_LANES), F32)],
        compiler_params=pltpu.CompilerParams(
            dimension_semantics=("parallel", "arbitrary"), vmem_limit_bytes=VMEM_LIMIT),
        name="rwkv_scan",
    )(r, k, v, lw, kk, a)


def _rwkv_out_kernel(h_ref, y_ref, bonus_ref, g_ref, gng_ref, gnb_ref, wo_ref, gpost_ref,
                     e_ref, et_ref, o_ref):
    e, et = e_ref[...], et_ref[...]
    y = y_ref[...]
    mean = _seg_sum(y, e, et) * (1.0 / HEAD)
    d = y - mean
    var = _seg_sum(d * d, e, et) * (1.0 / HEAD)
    yn = d * lax.rsqrt(var + GN_EPS) * gng_ref[...] + gnb_ref[...] + bonus_ref[...]
    m = _mm(yn * g_ref[...], wo_ref[...])
    o_ref[...] = h_ref[...] + _rms(m, gpost_ref[...])


def _rwkv_out(h, y, bonus, g, gn_g, gn_b, w_o, g_post, e, et, *, tm=512):
    m, d = h.shape
    tile = pl.BlockSpec((tm, d), lambda i: (i, 0))
    params = (gn_g, gn_b, w_o, g_post, e, et)
    return pl.pallas_call(
        _rwkv_out_kernel,
        out_shape=jax.ShapeDtypeStruct((m, d), F32),
        grid=(m // tm,),
        in_specs=[tile] * 4 + [_full_spec(p) for p in params],
        out_specs=tile,
        compiler_params=pltpu.CompilerParams(
            dimension_semantics=("parallel",), vmem_limit_bytes=VMEM_LIMIT),
        name="rwkv_out",
    )(h, y, bonus, g, *params)


def _kv_proj_kernel(h_ref, g_ref, w_ref, b_ref, o_ref):
    o_ref[...] = _mm(_rms(h_ref[...], g_ref[...]), w_ref[...]) + b_ref[...]


def _kv_proj(h, g, w, bias, *, tm=512):
    m, d = h.shape
    n = w.shape[1]
    return pl.pallas_call(
        _kv_proj_kernel,
        out_shape=jax.ShapeDtypeStruct((m, n), F32),
        grid=(m // tm,),
        in_specs=[pl.BlockSpec((tm, d), lambda i: (i, 0)), _full_spec(g), _full_spec(w), _full_spec(bias)],
        out_specs=pl.BlockSpec((tm, n), lambda i: (i, 0)),
        compiler_params=pltpu.CompilerParams(
            dimension_semantics=("parallel",), vmem_limit_bytes=VMEM_LIMIT),
        name="kv_proj",
    )(h, g, w, bias)


def _attn_kernel(sinks_ref, h_ref, kp_ref, kc_ref, vp_ref, vc_ref, gpre_ref, wq_ref, bq_ref,
                 wo_ref, bo_ref, gpost_ref, o_ref):
    nb = pl.program_id(1)
    h = h_ref[0]
    q = _mm(_rms(h, gpre_ref[...]), wq_ref[...]) + bq_ref[...]
    k_band = jnp.concatenate([kp_ref[0], kc_ref[0]], axis=0)
    v_band = jnp.concatenate([vp_ref[0], vc_ref[0]], axis=0)

    qi = lax.broadcasted_iota(jnp.int32, (WINDOW, 2 * WINDOW), 0)
    si = lax.broadcasted_iota(jnp.int32, (WINDOW, 2 * WINDOW), 1)
    valid = (si > qi) & (si <= qi + WINDOW) & ((si >= WINDOW) | (nb > 0))

    outs = []
    for kh in range(KV_HEADS):
        k_h = k_band[:, kh * HEAD:(kh + 1) * HEAD]
        v_h = v_band[:, kh * HEAD:(kh + 1) * HEAD]
        for gi in range(GROUP):
            hd = kh * GROUP + gi
            s = _mm_nt(q[:, hd * HEAD:(hd + 1) * HEAD], k_h) * (HEAD ** -0.5)
            s = jnp.where(valid, s, MASK_VALUE)
            sink = sinks_ref[hd]
            mx = jnp.maximum(jnp.max(s, axis=-1, keepdims=True), sink)
            p = jnp.exp(s - mx)
            denom = jnp.sum(p, axis=-1, keepdims=True) + jnp.exp(sink - mx)
            outs.append(_mm(p / denom, v_h))
    o = jnp.concatenate(outs, axis=1)
    m = _mm(o, wo_ref[...]) + bo_ref[...]
    o_ref[0] = h + _rms(m, gpost_ref[...])


def _attn_block(h3, kv3, sinks, g_pre, w_q, b_q, w_o, b_o, g_post):
    b, t, d = h3.shape
    kvw = KV_HEADS * HEAD
    tile = pl.BlockSpec((1, WINDOW, d), lambda i, j: (i, j, 0))
    params = (g_pre, w_q, b_q, w_o, b_o, g_post)
    return pl.pallas_call(
        _attn_kernel,
        out_shape=jax.ShapeDtypeStruct((b, t, d), F32),
        grid=(b, t // WINDOW),
        in_specs=[
            pl.BlockSpec(memory_space=pltpu.SMEM),
            tile,
            pl.BlockSpec((1, WINDOW, kvw), lambda i, j: (i, jnp.maximum(j - 1, 0), 0)),
            pl.BlockSpec((1, WINDOW, kvw), lambda i, j: (i, j, 0)),
            pl.BlockSpec((1, WINDOW, kvw), lambda i, j: (i, jnp.maximum(j - 1, 0), 1)),
            pl.BlockSpec((1, WINDOW, kvw), lambda i, j: (i, j, 1)),
        ] + [_full_spec(p) for p in params],
        out_specs=tile,
        compiler_params=pltpu.CompilerParams(
            dimension_semantics=("parallel", "parallel"), vmem_limit_bytes=VMEM_LIMIT),
        name="swa_block",
    )(sinks, h3, kv3, kv3, kv3, kv3, *params)


def kernel(x, norm_g, ffn_w_in, ffn_w_out, rwkv_mu, rwkv_w_rkv, rwkv_w_o, rwkv_w0, rwkv_w1, rwkv_w2, rwkv_a0, rwkv_a1, rwkv_a2, rwkv_g1, rwkv_g2, rwkv_k_k, rwkv_k_a, rwkv_r_k, rwkv_gn_g, rwkv_gn_b, kv_norm_g, w_kv, b_kv, attn_w_q, attn_b_q, attn_w_o, attn_b_o, attn_sinks):
    b, t, d = x.shape
    m = b * t
    depth = norm_g.shape[0]
    n_a = rwkv_mu.shape[0]
    row = lambda vec: vec.reshape(1, -1).astype(F32)
    bf = lambda w: w.astype(BF16)

    seg = (jnp.arange(d)[:, None] // HEAD == jnp.arange(SEG_PAD)[None, :]).astype(BF16)
    seg_t = seg.T

    h = x.reshape(m, d)
    kv = None
    for layer in range(depth):
        g = norm_g[layer]
        h = _ffn_block(h, row(g[0]), bf(ffn_w_in[layer, 0]), bf(ffn_w_out[layer, 0]), row(g[1]))
        if layer < n_a:
            i = layer
            params = (row(g[2]), rwkv_mu[i], bf(rwkv_w_rkv[i, 0]), bf(rwkv_w_rkv[i, 1]),
                      bf(rwkv_w_rkv[i, 2]), row(rwkv_w0[i]), bf(rwkv_w1[i]), bf(rwkv_w2[i]),
                      row(rwkv_a0[i]), bf(rwkv_a1[i]), bf(rwkv_a2[i]), bf(rwkv_g1[i]), bf(rwkv_g2[i]),
                      row(rwkv_k_k[i]), row(rwkv_k_a[i]), row(rwkv_r_k[i]), seg, seg_t)
            r, k, v, lw, kk, a, gate, bonus = _rwkv_prep(h.reshape(b, t, d), params)
            y = _rwkv_scan(r, k, v, lw, kk, a)
            h = _rwkv_out(h, y.reshape(m, d), bonus.reshape(m, d), gate.reshape(m, d),
                          row(rwkv_gn_g[i]), row(rwkv_gn_b[i]), bf(rwkv_w_o[i]), row(g[3]), seg, seg_t)
        else:
            j = layer - n_a
            h = _attn_block(h.reshape(b, t, d), kv, attn_sinks[j].astype(F32), row(g[2]),
                            bf(attn_w_q[j]), row(attn_b_q[j]), bf(attn_w_o[j]), row(attn_b_o[j]),
                            row(g[3])).reshape(m, d)
        h = _ffn_block(h, row(g[4]), bf(ffn_w_in[layer, 1]), bf(ffn_w_out[layer, 1]), row(g[5]))
        if layer == n_a - 1:
            kv = _kv_proj(h, row(kv_norm_g), bf(w_kv), row(b_kv)).reshape(b, t, -1)
    return h.reshape(b, t, d)
```

```python
import jax
import jax.numpy as jnp
from jax import lax
from jax.experimental import pallas as pl
from jax.experimental.pallas import tpu as pltpu

F32 = jnp.float32
BF16 = jnp.bfloat16

RMS_EPS = 1e-6
GN_EPS = 64e-5
HEAD = 64
WINDOW = 128
MASK_VALUE = -1e30
KV_HEADS = 4
GROUP = 4
ATTN_BLOCKS = 4
LOG2E = 1.4426950408889634

CHUNK = 64
GROUP_LANES = 256
HEADS_PER_GROUP = GROUP_LANES // HEAD
BASE_BLOCK = 8
SEG_PAD = 128

VMEM_LIMIT = 56 * 1024 * 1024
MXU_WIDTH = 256


def _rms(x, g):
    return x * lax.rsqrt(jnp.mean(x * x, axis=-1, keepdims=True) + RMS_EPS) * g


def _mm(a, b):
    return jnp.dot(a.astype(BF16), b.astype(BF16), preferred_element_type=F32)


def _mm_nt(a, b):
    return lax.dot_general(a.astype(BF16), b.astype(BF16), (((1,), (1,)), ((), ())),
                           preferred_element_type=F32)


def _split2(x):
    hi = x.astype(BF16)
    lo = (x - hi.astype(F32)).astype(BF16)
    return hi, lo


def _split3(x):
    h1 = x.astype(BF16)
    r1 = x - h1.astype(F32)
    h2 = r1.astype(BF16)
    h3 = (r1 - h2.astype(F32)).astype(BF16)
    return h1, h2, h3


def _seg_sum(x, e, et):
    s = jnp.dot(x.astype(BF16), e, preferred_element_type=F32)
    shi, slo = _split2(s)
    return jnp.dot(shi, et, preferred_element_type=F32) + jnp.dot(slo, et, preferred_element_type=F32)


def _full_spec(x):
    return pl.BlockSpec(x.shape, lambda *_: (0,) * x.ndim)


def _ffn_chunks(d_ff):
    tiles = d_ff // MXU_WIDTH
    first = (tiles + 1) // 2 * MXU_WIDTH
    return ((0, first), (first, d_ff))


def _ffn_kernel(h_ref, gpre_ref, win_ref, wo_ref, gpost_ref, o_ref):
    d_ff = wo_ref.shape[0]
    h = h_ref[...]
    xn = _rms(h, gpre_ref[...]).astype(BF16)
    acc = None
    for lo, hi in _ffn_chunks(d_ff):
        gate = jnp.dot(xn, win_ref[:, lo:hi], preferred_element_type=F32)
        up = jnp.dot(xn, win_ref[:, d_ff + lo:d_ff + hi], preferred_element_type=F32)
        act = (gate * jax.nn.sigmoid(gate) * up).astype(BF16)
        part = jnp.dot(act, wo_ref[lo:hi, :], preferred_element_type=F32)
        acc = part if acc is None else acc + part
    o_ref[...] = h + 0.5 * _rms(acc, gpost_ref[...])


def _ffn_block(h, g_pre, w_in, w_out, g_post, layer, which, *, tm=512):
    m, d = h.shape
    resident = lambda x: pl.BlockSpec(x.shape, lambda i: (0,) * x.ndim, pipeline_mode=pl.Buffered(1))
    picked = lambda x: pl.BlockSpec((None, None) + x.shape[2:], lambda i: (layer, which, 0, 0),
                                    pipeline_mode=pl.Buffered(1))
    return pl.pallas_call(
        _ffn_kernel,
        out_shape=jax.ShapeDtypeStruct((m, d), F32),
        grid=(m // tm,),
        in_specs=[pl.BlockSpec((tm, d), lambda i: (i, 0)), resident(g_pre), picked(w_in),
                  picked(w_out), resident(g_post)],
        out_specs=pl.BlockSpec((tm, d), lambda i: (i, 0)),
        compiler_params=pltpu.CompilerParams(
            dimension_semantics=("parallel",), vmem_limit_bytes=VMEM_LIMIT),
        name="ffn_block",
    )(h, g_pre, w_in, w_out, g_post)


def _rwkv_prep_kernel(h_ref, g_ref, mu_ref, wrkv_ref, w0_ref, w1_ref, w2_ref,
                      a0_ref, a1_ref, a2_ref, g1_ref, g2_ref, kk_ref, ka_ref, rk_ref,
                      e_ref, et_ref,
                      r_out, k_out, v_out, lw_out, kk_out, a_out, g_out, bonus_out,
                      carry_ref):
    t = pl.program_id(1)
    u = _rms(h_ref[0], g_ref[...])
    tm = u.shape[0]

    @pl.when(t == 0)
    def _():
        carry_ref[...] = jnp.zeros_like(carry_ref)

    row = lax.broadcasted_iota(jnp.int32, u.shape, 0)
    prev = jnp.where(row == 0, carry_ref[0:1, :], pltpu.roll(u, shift=1, axis=0))
    carry_ref[0:1, :] = u[tm - 1:tm, :]
    xx = prev - u
    mu = mu_ref[...]
    xr, xw, xk, xv, xa, xg = [u + xx * mu[i:i + 1, :] for i in range(6)]

    r = _mm(xr, wrkv_ref[0])
    k = _mm(xk, wrkv_ref[1])
    v = _mm(xv, wrkv_ref[2])
    z = -(w0_ref[...] + _mm(jnp.tanh(_mm(xw, w1_ref[...])), w2_ref[...]))
    softplus = jnp.maximum(z, 0.0) + jnp.log1p(jnp.exp(-jnp.abs(z)))
    w = -softplus - 0.5
    a = jax.nn.sigmoid(a0_ref[...] + _mm(_mm(xa, a1_ref[...]), a2_ref[...]))
    g = _mm(jax.nn.sigmoid(_mm(xg, g1_ref[...])), g2_ref[...])

    e, et = e_ref[...], et_ref[...]
    kk = k * kk_ref[...]
    norm = jnp.sqrt(_seg_sum(kk * kk, e, et))
    kk = kk / jnp.maximum(norm, 1e-12)
    k = k * (1.0 + (a - 1.0) * ka_ref[...])

    r_out[0] = r
    k_out[0] = k
    v_out[0] = v
    lw_out[0] = -jnp.exp(w)
    kk_out[0] = kk
    a_out[0] = a
    g_out[0] = g
    bonus_out[0] = _seg_sum(r * k * rk_ref[...], e, et) * v


def _rwkv_prep(h3, params, *, tm=256):
    b, t, d = h3.shape
    tile = pl.BlockSpec((1, tm, d), lambda i, j: (i, j, 0))
    return pl.pallas_call(
        _rwkv_prep_kernel,
        out_shape=[jax.ShapeDtypeStruct((b, t, d), F32)] * 8,
        grid=(b, t // tm),
        in_specs=[tile] + [_full_spec(p) for p in params],
        out_specs=[tile] * 8,
        scratch_shapes=[pltpu.VMEM((8, d), F32)],
        compiler_params=pltpu.CompilerParams(
            dimension_semantics=("parallel", "arbitrary"), vmem_limit_bytes=VMEM_LIMIT),
        name="rwkv_prep",
    )(h3, *params)


def _scan_masks():
    c, n = CHUNK, GROUP_LANES
    row_s = lax.broadcasted_iota(jnp.int32, (c, n), 0)
    col_s = lax.broadcasted_iota(jnp.int32, (c, n), 1) % c
    strict = row_s > col_s
    incl = row_s >= col_s
    eye = row_s == col_s
    base = strict & ((row_s // BASE_BLOCK) == (col_s // BASE_BLOCK))
    offs = []
    b = BASE_BLOCK
    while b < c:
        offs.append(((row_s // (2 * b)) == (col_s // (2 * b)))
                    & ((row_s // b) % 2 == 1) & ((col_s // b) % 2 == 0))
        b *= 2
    row_b = lax.broadcasted_iota(jnp.int32, (n, n), 0)
    col_b = lax.broadcasted_iota(jnp.int32, (n, n), 1)
    mask_bd = (row_b // HEAD) == (col_b // HEAD)
    tri_ones = jnp.where(lax.broadcasted_iota(jnp.int32, (c, c), 0)
                         >= lax.broadcasted_iota(jnp.int32, (c, c), 1), 1.0, 0.0).astype(BF16)
    return mask_bd, strict, incl, eye, base, tuple(offs), tri_ones


def _block_diag(z, mask_bd):
    tiled = jnp.concatenate([z] * HEADS_PER_GROUP, axis=0)
    return jnp.where(mask_bd, tiled, 0.0)


def _head_transpose(x):
    xt = x.T
    return jnp.concatenate([xt[h * HEAD:(h + 1) * HEAD, :] for h in range(HEADS_PER_GROUP)], axis=1)


def _scan_groups(r, k, v, lw, kk, a, s_cat, masks):
    mask_bd, strict, incl, eye, base, offs, tri_ones = masks
    c, n = CHUNK, GROUP_LANES
    groups = range(len(r))
    bd = lambda z: _block_diag(z, mask_bd)
    cat0 = lambda *xs: jnp.concatenate(xs, axis=0)
    cat1 = lambda *xs: jnp.concatenate(xs, axis=1)

    l_cum = [sum(jnp.dot(tri_ones, piece, preferred_element_type=F32) for piece in _split2(lw[g]))
             for g in groups]
    e_l = [jnp.exp(l_cum[g]) for g in groups]
    e_nl = [jnp.exp(-l_cum[g]) for g in groups]
    a_t = [-kk[g] * jnp.exp(l_cum[g] - lw[g]) for g in groups]
    r_t = [r[g] * e_l[g] for g in groups]
    b_t = [kk[g] * a[g] * e_nl[g] for g in groups]
    k_t = [k[g] * e_nl[g] for g in groups]
    p_end = [e_l[g][c - 1:c, :] for g in groups]
    b_ht = [_head_transpose(b_t[g] * p_end[g]) for g in groups]
    k_ht = [_head_transpose(k_t[g] * p_end[g]) for g in groups]

    a_all = [_mm_nt(cat0(a_t[g], r_t[g]), cat0(bd(b_t[g]), bd(k_t[g]))) for g in groups]
    a_ab = [a_all[g][:c, :n] for g in groups]

    p = [jnp.where(base, a_ab[g], 0.0) for g in groups]
    inv = [jnp.where(eye, 1.0, 0.0) + p[g] for g in groups]
    p = [_mm(p[g], bd(p[g])) for g in groups]
    for _ in range(BASE_BLOCK.bit_length() - 3):
        both = [_mm(cat0(p[g], inv[g]), bd(p[g])) for g in groups]
        p = [both[g][:c] for g in groups]
        inv = [inv[g] + both[g][c:] for g in groups]
    inv = [inv[g] + _mm(inv[g], bd(p[g])) for g in groups]
    for off in offs:
        x = [_mm(jnp.where(off, a_ab[g], 0.0), bd(inv[g])) for g in groups]
        inv = [inv[g] + _mm(inv[g], bd(x[g])) for g in groups]

    av = [_mm(cat0(jnp.where(strict, a_all[g][:c, n:], 0.0), jnp.where(incl, a_all[g][c:, n:], 0.0),
                   k_ht[g]), bd(v[g])) for g in groups]
    wz = [_mm(inv[g], cat1(bd(a_t[g]), bd(av[g][:c]))) for g in groups]
    qg = [_mm(cat0(jnp.where(incl, a_all[g][c:, :n], 0.0), b_ht[g]),
              cat1(bd(wz[g][:, :n]), bd(wz[g][:, n:]))) for g in groups]
    g_cat = [qg[g][c:, :n] + jnp.where(eye, p_end[g], 0.0) for g in groups]
    sy = [_mm(cat0(g_cat[g], r_t[g] + qg[g][:c, :n]), bd(s_cat[g])) for g in groups]
    y = [sy[g][c:] + qg[g][:c, n:] + av[g][c:2 * c] for g in groups]
    s_new = [sy[g][:c] + qg[g][c:, n:] + av[g][2 * c:] for g in groups]
    return y, s_new


def _rwkv_scan_kernel(r_ref, k_ref, v_ref, lw_ref, kk_ref, a_ref, y_ref, s_ref):
    @pl.when(pl.program_id(1) == 0)
    def _():
        s_ref[...] = jnp.zeros_like(s_ref)

    nb, _, d = r_ref.shape
    where = [(bi, slice(lo, lo + GROUP_LANES)) for bi in range(nb) for lo in range(0, d, GROUP_LANES)]
    load = lambda ref: [ref[bi, :, sl] for bi, sl in where]
    y, s_new = _scan_groups(load(r_ref), load(k_ref), load(v_ref), load(lw_ref), load(kk_ref),
                            load(a_ref), [s_ref[g] for g in range(len(where))], _scan_masks())
    for g, (bi, sl) in enumerate(where):
        y_ref[bi, :, sl] = y[g]
        s_ref[g] = s_new[g]


def _rwkv_scan(r, k, v, lw, kk, a, *, batch_per_step=4):
    b, t, d = r.shape
    nb = batch_per_step
    tile = pl.BlockSpec((nb, CHUNK, d), lambda i, j: (i, j, 0))
    return pl.pallas_call(
        _rwkv_scan_kernel,
        out_shape=jax.ShapeDtypeStruct((b, t, d), F32),
        grid=(b // nb, t // CHUNK),
        in_specs=[tile] * 6,
        out_specs=tile,
        scratch_shapes=[pltpu.VMEM((nb * d // GROUP_LANES, HEAD, GROUP_LANES), F32)],
        compiler_params=pltpu.CompilerParams(
            dimension_semantics=("parallel", "arbitrary"), vmem_limit_bytes=VMEM_LIMIT),
        name="rwkv_scan",
    )(r, k, v, lw, kk, a)


def _rwkv_out_kernel(h_ref, y_ref, bonus_ref, g_ref, gng_ref, gnb_ref, wo_ref, gpost_ref,
                     e_ref, et_ref, o_ref):
    e, et = e_ref[...], et_ref[...]
    y = y_ref[...]
    mean = _seg_sum(y, e, et) * (1.0 / HEAD)
    d = y - mean
    var = _seg_sum(d * d, e, et) * (1.0 / HEAD)
    yn = d * lax.rsqrt(var + GN_EPS) * gng_ref[...] + gnb_ref[...] + bonus_ref[...]
    m = _mm(yn * g_ref[...], wo_ref[...])
    o_ref[...] = h_ref[...] + _rms(m, gpost_ref[...])


def _rwkv_out(h, y, bonus, g, gn_g, gn_b, w_o, g_post, e, et, *, tm=512):
    m, d = h.shape
    tile = pl.BlockSpec((tm, d), lambda i: (i, 0))
    params = (gn_g, gn_b, w_o, g_post, e, et)
    return pl.pallas_call(
        _rwkv_out_kernel,
        out_shape=jax.ShapeDtypeStruct((m, d), F32),
        grid=(m // tm,),
        in_specs=[tile] * 4 + [_full_spec(p) for p in params],
        out_specs=tile,
        compiler_params=pltpu.CompilerParams(
            dimension_semantics=("parallel",), vmem_limit_bytes=VMEM_LIMIT),
        name="rwkv_out",
    )(h, y, bonus, g, *params)


def _kv_proj_kernel(h_ref, g_ref, wk_ref, bk_ref, wvt_ref, bvt_ref, k_ref, vt_ref):
    u = _rms(h_ref[0], g_ref[...]).astype(BF16)
    k_ref[0] = (jnp.dot(u, wk_ref[...], preferred_element_type=F32) + bk_ref[...]).astype(BF16)
    vt_ref[0] = (_mm_nt(wvt_ref[...], u) + bvt_ref[...]).astype(BF16)


def _kv_proj(h3, g, wk, bk, wvt, bvt, *, tm=512):
    b, t, d = h3.shape
    kvw = wk.shape[1]
    params = (g, wk, bk, wvt, bvt)
    return pl.pallas_call(
        _kv_proj_kernel,
        out_shape=[jax.ShapeDtypeStruct((b, t, kvw), BF16), jax.ShapeDtypeStruct((b, kvw, t), BF16)],
        grid=(b, t // tm),
        in_specs=[pl.BlockSpec((1, tm, d), lambda i, j: (i, j, 0))] + [_full_spec(p) for p in params],
        out_specs=[pl.BlockSpec((1, tm, kvw), lambda i, j: (i, j, 0)),
                   pl.BlockSpec((1, kvw, tm), lambda i, j: (i, 0, j))],
        compiler_params=pltpu.CompilerParams(
            dimension_semantics=("parallel", "parallel"), vmem_limit_bytes=VMEM_LIMIT),
        name="kv_proj",
    )(h3, *params)


def _attn_kernel(sinks_ref, h_ref, kp_ref, kc_ref, vtp_ref, vtc_ref, gpre_ref, wqt_ref, bqt_ref,
                 wo_ref, bo_ref, gpost_ref, o_ref):
    step = pl.program_id(1)
    w = WINDOW
    h = h_ref[0]
    u = _rms(h, gpre_ref[...])
    qt = ((_mm_nt(wqt_ref[...], u) + bqt_ref[...]) * (LOG2E * HEAD ** -0.5)).astype(BF16)
    k_cur, k_prev = kc_ref[0], kp_ref[0]
    vt_cur, vt_prev = vtc_ref[0], vtp_ref[0]
    kvw = k_cur.shape[1]
    lanes = GROUP * w

    si = lax.broadcasted_iota(jnp.int32, (2 * w, lanes), 0)
    qi = lax.broadcasted_iota(jnp.int32, (2 * w, lanes), 1) % w
    valid = (si > qi) & (si <= qi + w)
    valid_first = valid & ((si >= w) | (step > 0))
    seg = lax.broadcasted_iota(jnp.int32, (1, lanes), 1) // w

    def scores(n):
        k_band = (jnp.concatenate([k_prev, k_cur[:w]], axis=0) if n == 0
                  else k_cur[(n - 1) * w:(n + 1) * w])
        out = []
        for kh in range(KV_HEADS):
            q_stack = jnp.concatenate(
                [qt[(kh * GROUP + g) * HEAD:(kh * GROUP + g + 1) * HEAD, n * w:(n + 1) * w]
                 for g in range(GROUP)], axis=1)
            pieces = [q_stack]
            if kh > 0:
                pieces.insert(0, jnp.zeros((kh * HEAD, lanes), BF16))
            if kh < KV_HEADS - 1:
                pieces.append(jnp.zeros(((KV_HEADS - 1 - kh) * HEAD, lanes), BF16))
            out.append(jnp.dot(k_band, jnp.concatenate(pieces, axis=0), preferred_element_type=F32))
        return out

    def softmax(n, st):
        mask = valid_first if n == 0 else valid
        out = []
        for kh in range(KV_HEADS):
            sink = sinks_ref[kh * GROUP] * LOG2E
            for g in range(1, GROUP):
                sink = jnp.where(seg == g, sinks_ref[kh * GROUP + g] * LOG2E, sink)
            s = jnp.where(mask, st[kh], MASK_VALUE)
            mx = jnp.maximum(jnp.max(s, axis=0, keepdims=True), sink)
            p = jnp.exp2(s - mx)
            denom = jnp.sum(p, axis=0, keepdims=True) + jnp.exp2(sink - mx)
            out.append((p.astype(BF16), 1.0 / denom))
        return out

    def values(n, pd):
        vt_band = (jnp.concatenate([vt_prev, vt_cur[:, :w]], axis=1) if n == 0
                   else vt_cur[:, (n - 1) * w:(n + 1) * w])
        heads = []
        for kh in range(KV_HEADS):
            p, inv_denom = pd[kh]
            ot = jnp.dot(vt_band[kh * HEAD:(kh + 1) * HEAD], p, preferred_element_type=F32) * inv_denom
            heads += [ot[:, g * w:(g + 1) * w] for g in range(GROUP)]
        return jnp.concatenate(heads, axis=0).astype(BF16)

    st = [scores(0)]
    cols = []
    for n in range(ATTN_BLOCKS):
        if n + 1 < ATTN_BLOCKS:
            st.append(scores(n + 1))
        cols.append(values(n, softmax(n, st[n])))
    ot_all = jnp.concatenate(cols, axis=1)
    m = lax.dot_general(ot_all, wo_ref[...], (((0,), (0,)), ((), ())),
                        preferred_element_type=F32) + bo_ref[...]
    o_ref[0] = h + _rms(m, gpost_ref[...])


def _attn_block(h3, k3, vt3, sinks, g_pre, w_qt, b_qt, w_o, b_o, g_post):
    b, t, d = h3.shape
    kvw = k3.shape[-1]
    tq = ATTN_BLOCKS * WINDOW
    tile = pl.BlockSpec((1, tq, d), lambda i, j: (i, j, 0))
    prev = lambda j: jnp.maximum(j * ATTN_BLOCKS - 1, 0)
    params = (g_pre, w_qt, b_qt, w_o, b_o, g_post)
    return pl.pallas_call(
        _attn_kernel,
        out_shape=jax.ShapeDtypeStruct((b, t, d), F32),
        grid=(b, t // tq),
        in_specs=[
            pl.BlockSpec(memory_space=pltpu.SMEM),
            tile,
            pl.BlockSpec((1, WINDOW, kvw), lambda i, j: (i, prev(j), 0)),
            pl.BlockSpec((1, tq, kvw), lambda i, j: (i, j, 0)),
            pl.BlockSpec((1, kvw, WINDOW), lambda i, j: (i, 0, prev(j))),
            pl.BlockSpec((1, kvw, tq), lambda i, j: (i, 0, j)),
        ] + [_full_spec(p) for p in params],
        out_specs=tile,
        compiler_params=pltpu.CompilerParams(
            dimension_semantics=("parallel", "parallel"), vmem_limit_bytes=VMEM_LIMIT),
        name="swa_block",
    )(sinks, h3, k3, k3, vt3, vt3, *params)


def kernel(x, norm_g, ffn_w_in, ffn_w_out, rwkv_mu, rwkv_w_rkv, rwkv_w_o, rwkv_w0, rwkv_w1, rwkv_w2, rwkv_a0, rwkv_a1, rwkv_a2, rwkv_g1, rwkv_g2, rwkv_k_k, rwkv_k_a, rwkv_r_k, rwkv_gn_g, rwkv_gn_b, kv_norm_g, w_kv, b_kv, attn_w_q, attn_b_q, attn_w_o, attn_b_o, attn_sinks):
    b, t, d = x.shape
    m = b * t
    depth = norm_g.shape[0]
    n_a = rwkv_mu.shape[0]
    row = lambda vec: vec.reshape(1, -1).astype(F32)
    col = lambda vec: vec.reshape(-1, 1).astype(F32)
    bf = lambda w: w.astype(BF16)

    seg = (jnp.arange(d)[:, None] // HEAD == jnp.arange(SEG_PAD)[None, :]).astype(BF16)
    seg_t = seg.T

    h = x.reshape(m, d)
    k_sh = vt_sh = None
    w_in, w_out = bf(ffn_w_in), bf(ffn_w_out)
    for layer in range(depth):
        g = norm_g[layer]
        h = _ffn_block(h, row(g[0]), w_in, w_out, row(g[1]), layer, 0)
        if layer < n_a:
            i = layer
            params = (row(g[2]), rwkv_mu[i], bf(rwkv_w_rkv[i]),
                      row(rwkv_w0[i]), bf(rwkv_w1[i]), bf(rwkv_w2[i]),
                      row(rwkv_a0[i]), bf(rwkv_a1[i]), bf(rwkv_a2[i]), bf(rwkv_g1[i]), bf(rwkv_g2[i]),
                      row(rwkv_k_k[i]), row(rwkv_k_a[i]), row(rwkv_r_k[i]), seg, seg_t)
            r, k, v, lw, kk, a, gate, bonus = _rwkv_prep(h.reshape(b, t, d), params)
            y = _rwkv_scan(r, k, v, lw, kk, a)
            h = _rwkv_out(h, y.reshape(m, d), bonus.reshape(m, d), gate.reshape(m, d),
                          row(rwkv_gn_g[i]), row(rwkv_gn_b[i]), bf(rwkv_w_o[i]), row(g[3]), seg, seg_t)
        else:
            j = layer - n_a
            h = _attn_block(h.reshape(b, t, d), k_sh, vt_sh, attn_sinks[j].astype(F32), row(g[2]),
                            bf(attn_w_q[j].T), col(attn_b_q[j]), bf(attn_w_o[j]), row(attn_b_o[j]),
                            row(g[3])).reshape(m, d)
        h = _ffn_block(h, row(g[4]), w_in, w_out, row(g[5]), layer, 1)
        if layer == n_a - 1:
            kvw = w_kv.shape[1] // 2
            k_sh, vt_sh = _kv_proj(h.reshape(b, t, d), row(kv_norm_g), bf(w_kv[:, :kvw]), row(b_kv[:kvw]),
                                   bf(w_kv[:, kvw:].T), col(b_kv[kvw:]))
    return h.reshape(b, t, d)
```

```python
import jax
import jax.numpy as jnp
from jax import lax
from jax.experimental import pallas as pl
from jax.experimental.pallas import tpu as pltpu

F32 = jnp.float32
BF16 = jnp.bfloat16

RMS_EPS = 1e-6
GN_EPS = 64e-5
HEAD = 64
WINDOW = 128
MASK_VALUE = -1e30
KV_HEADS = 4
GROUP = 4
ATTN_BLOCKS = 4
LOG2E = 1.4426950408889634

CHUNK = 64
GROUP_LANES = 256
HEADS_PER_GROUP = GROUP_LANES // HEAD
BASE_BLOCK = 8
PREP_ROWS = 256
FFN_ROWS = 512
EXP_MINUS_HALF = 0.6065306597126334

VMEM_LIMIT = 56 * 1024 * 1024
MXU_WIDTH = 256


def _rms(x, g):
    return x * lax.rsqrt(jnp.mean(x * x, axis=-1, keepdims=True) + RMS_EPS) * g


def _mm(a, b):
    return jnp.dot(a.astype(BF16), b.astype(BF16), preferred_element_type=F32)


def _mm_nt(a, b):
    return lax.dot_general(a.astype(BF16), b.astype(BF16), (((1,), (1,)), ((), ())),
                           preferred_element_type=F32)


def _split2(x):
    hi = x.astype(BF16)
    lo = (x - hi.astype(F32)).astype(BF16)
    return hi, lo


def _split3(x):
    h1 = x.astype(BF16)
    r1 = x - h1.astype(F32)
    h2 = r1.astype(BF16)
    h3 = (r1 - h2.astype(F32)).astype(BF16)
    return h1, h2, h3


def _seg_sum(x, ones_bd, pieces=1):
    out = []
    for lo in range(0, x.shape[1], GROUP_LANES):
        xs = x[:, lo:lo + GROUP_LANES]
        parts = (xs.astype(BF16),) if pieces == 1 else _split2(xs)
        out.append(sum(jnp.dot(p, ones_bd, preferred_element_type=F32) for p in parts))
    return jnp.concatenate(out, axis=1)


def _full_spec(x):
    return pl.BlockSpec(x.shape, lambda *_: (0,) * x.ndim)


def _ffn_chunks(d_ff):
    tiles = d_ff // MXU_WIDTH
    first = (tiles + 1) // 2 * MXU_WIDTH
    return ((0, first), (first, d_ff))


def _ffn_kernel(h_ref, gpre_ref, win_ref, wo_ref, gpost_ref, o_ref):
    d_ff = wo_ref.shape[0]
    for r0 in range(0, h_ref.shape[0], FFN_ROWS):
        rows = slice(r0, r0 + FFN_ROWS)
        h = h_ref[rows, :]
        xn = _rms(h, gpre_ref[...]).astype(BF16)
        acc = None
        for lo, hi in _ffn_chunks(d_ff):
            gate = jnp.dot(xn, win_ref[:, lo:hi], preferred_element_type=F32)
            up = jnp.dot(xn, win_ref[:, d_ff + lo:d_ff + hi], preferred_element_type=F32)
            act = (gate * jax.nn.sigmoid(gate) * up).astype(BF16)
            part = jnp.dot(act, wo_ref[lo:hi, :], preferred_element_type=F32)
            acc = part if acc is None else acc + part
        o_ref[rows, :] = h + 0.5 * _rms(acc, gpost_ref[...])


def _ffn_block(h, g_pre, w_in, w_out, g_post, layer, which, *, tm=1024):
    m, d = h.shape
    resident = lambda x: pl.BlockSpec(x.shape, lambda i: (0,) * x.ndim, pipeline_mode=pl.Buffered(1))
    picked = lambda x: pl.BlockSpec((None, None) + x.shape[2:], lambda i: (layer, which, 0, 0),
                                    pipeline_mode=pl.Buffered(1))
    return pl.pallas_call(
        _ffn_kernel,
        out_shape=jax.ShapeDtypeStruct((m, d), F32),
        grid=(m // tm,),
        in_specs=[pl.BlockSpec((tm, d), lambda i: (i, 0)), resident(g_pre), picked(w_in),
                  picked(w_out), resident(g_post)],
        out_specs=pl.BlockSpec((tm, d), lambda i: (i, 0)),
        compiler_params=pltpu.CompilerParams(
            dimension_semantics=("parallel",), vmem_limit_bytes=VMEM_LIMIT),
        name="ffn_block",
    )(h, g_pre, w_in, w_out, g_post)


def _rwkv_prep_kernel(h_ref, g_ref, mu_ref, wrkv_ref, w0_ref, w1_ref, w2_ref,
                      a0_ref, a1_ref, a2_ref, g1_ref, g2_ref, kk_ref, ka_ref, rk_ref,
                      ones_ref,
                      r_out, k_out, v_out, lw_out, kk_out, a_out, g_out, bonus_out,
                      carry_ref):
    @pl.when(pl.program_id(1) == 0)
    def _():
        carry_ref[...] = jnp.zeros_like(carry_ref)

    ones_bd = ones_ref[...]
    mu = mu_ref[...]
    tm = h_ref.shape[1]
    row = lax.broadcasted_iota(jnp.int32, (PREP_ROWS, h_ref.shape[2]), 0)
    last = carry_ref[0:1, :]
    for r0 in range(0, tm, PREP_ROWS):
        rows = slice(r0, r0 + PREP_ROWS)
        u = _rms(h_ref[0, rows, :], g_ref[...])
        prev = jnp.where(row == 0, last, pltpu.roll(u, shift=1, axis=0))
        last = u[PREP_ROWS - 1:PREP_ROWS, :]
        xx = prev - u
        xr, xw, xk, xv, xa, xg = [u + xx * mu[i:i + 1, :] for i in range(6)]

        r = _mm(xr, wrkv_ref[0])
        k = _mm(xk, wrkv_ref[1])
        v = _mm(xv, wrkv_ref[2])
        x = w0_ref[...] + _mm(jnp.tanh(_mm(xw, w1_ref[...])), w2_ref[...])
        lw = jax.nn.sigmoid(x) * (-EXP_MINUS_HALF)
        a = jax.nn.sigmoid(a0_ref[...] + _mm(_mm(xa, a1_ref[...]), a2_ref[...]))
        g = _mm(jax.nn.sigmoid(_mm(xg, g1_ref[...])), g2_ref[...])

        kk = k * kk_ref[...]
        norm = jnp.sqrt(_seg_sum(kk * kk, ones_bd))
        kk = kk / jnp.maximum(norm, 1e-12)
        k = k * (1.0 + (a - 1.0) * ka_ref[...])

        r_out[0, rows, :] = r
        k_out[0, rows, :] = k
        v_out[0, rows, :] = v.astype(BF16)
        lw_out[0, rows, :] = lw
        kk_out[0, rows, :] = kk
        a_out[0, rows, :] = a
        g_out[0, rows, :] = g
        bonus_out[0, rows, :] = _seg_sum(r * k * rk_ref[...], ones_bd) * v
    carry_ref[0:1, :] = last


def _rwkv_prep(h3, params, *, tm=512):
    b, t, d = h3.shape
    tile = pl.BlockSpec((1, tm, d), lambda i, j: (i, j, 0))
    resident = lambda x: pl.BlockSpec(x.shape, lambda i, j: (0,) * x.ndim, pipeline_mode=pl.Buffered(1))
    dtypes = [F32, F32, BF16, F32, F32, F32, F32, F32]
    return pl.pallas_call(
        _rwkv_prep_kernel,
        out_shape=[jax.ShapeDtypeStruct((b, t, d), dt) for dt in dtypes],
        grid=(b, t // tm),
        in_specs=[tile] + [resident(p) for p in params],
        out_specs=[tile] * 8,
        scratch_shapes=[pltpu.VMEM((8, d), F32)],
        compiler_params=pltpu.CompilerParams(
            dimension_semantics=("parallel", "arbitrary"), vmem_limit_bytes=VMEM_LIMIT),
        name="rwkv_prep",
    )(h3, *params)


def _scan_masks():
    c, n = CHUNK, GROUP_LANES
    row_s = lax.broadcasted_iota(jnp.int32, (c, n), 0)
    col_s = lax.broadcasted_iota(jnp.int32, (c, n), 1) % c
    strict = row_s > col_s
    incl = row_s >= col_s
    eye = row_s == col_s
    base = strict & ((row_s // BASE_BLOCK) == (col_s // BASE_BLOCK))
    offs = []
    b = BASE_BLOCK
    while b < c:
        offs.append(((row_s // (2 * b)) == (col_s // (2 * b)))
                    & ((row_s // b) % 2 == 1) & ((col_s // b) % 2 == 0))
        b *= 2
    row_b = lax.broadcasted_iota(jnp.int32, (n, n), 0)
    col_b = lax.broadcasted_iota(jnp.int32, (n, n), 1)
    mask_bd = (row_b // HEAD) == (col_b // HEAD)
    tri_ones = jnp.where(lax.broadcasted_iota(jnp.int32, (c, c), 0)
                         >= lax.broadcasted_iota(jnp.int32, (c, c), 1), 1.0, 0.0).astype(BF16)
    return mask_bd, strict, incl, eye, base, tuple(offs), tri_ones


def _block_diag(z, mask_bd):
    tiled = jnp.concatenate([z] * HEADS_PER_GROUP, axis=0)
    return jnp.where(mask_bd, tiled, 0.0)


def _head_transpose(x):
    xt = x.T
    return jnp.concatenate([xt[h * HEAD:(h + 1) * HEAD, :] for h in range(HEADS_PER_GROUP)], axis=1)


def _scan_groups(r, k, v, lw, kk, a, s_cat, masks):
    mask_bd, strict, incl, eye, base, offs, tri_ones = masks
    c, n = CHUNK, GROUP_LANES
    groups = range(len(r))
    bd = lambda z: _block_diag(z, mask_bd)
    cat0 = lambda *xs: jnp.concatenate(xs, axis=0)
    cat1 = lambda *xs: jnp.concatenate(xs, axis=1)

    l_cum = [sum(jnp.dot(tri_ones, piece, preferred_element_type=F32) for piece in _split2(lw[g]))
             for g in groups]
    e_l = [jnp.exp(l_cum[g]) for g in groups]
    e_nl = [jnp.exp(-l_cum[g]) for g in groups]
    a_t = [-kk[g] * jnp.exp(l_cum[g] - lw[g]) for g in groups]
    r_t = [r[g] * e_l[g] for g in groups]
    b_t = [kk[g] * a[g] * e_nl[g] for g in groups]
    k_t = [k[g] * e_nl[g] for g in groups]
    p_end = [e_l[g][c - 1:c, :] for g in groups]
    b_ht = [_head_transpose(b_t[g] * p_end[g]) for g in groups]
    k_ht = [_head_transpose(k_t[g] * p_end[g]) for g in groups]

    a_all = [_mm_nt(cat0(a_t[g], r_t[g]), cat0(bd(b_t[g]), bd(k_t[g]))) for g in groups]
    a_ab = [a_all[g][:c, :n] for g in groups]

    p = [jnp.where(base, a_ab[g], 0.0) for g in groups]
    inv = [jnp.where(eye, 1.0, 0.0) + p[g] for g in groups]
    p = [_mm(p[g], bd(p[g])) for g in groups]
    for _ in range(BASE_BLOCK.bit_length() - 3):
        both = [_mm(cat0(p[g], inv[g]), bd(p[g])) for g in groups]
        p = [both[g][:c] for g in groups]
        inv = [inv[g] + both[g][c:] for g in groups]
    inv = [inv[g] + _mm(inv[g], bd(p[g])) for g in groups]
    for off in offs:
        x = [_mm(jnp.where(off, a_ab[g], 0.0), bd(inv[g])) for g in groups]
        inv = [inv[g] + _mm(inv[g], bd(x[g])) for g in groups]

    av = [_mm(cat0(jnp.where(strict, a_all[g][:c, n:], 0.0), jnp.where(incl, a_all[g][c:, n:], 0.0),
                   k_ht[g]), bd(v[g])) for g in groups]
    wz = [_mm(inv[g], cat1(bd(a_t[g]), bd(av[g][:c]))) for g in groups]
    qg = [_mm(cat0(jnp.where(incl, a_all[g][c:, :n], 0.0), b_ht[g]),
              cat1(bd(wz[g][:, :n]), bd(wz[g][:, n:]))) for g in groups]
    g_cat = [qg[g][c:, :n] + jnp.where(eye, p_end[g], 0.0) for g in groups]
    sy = [_mm(cat0(g_cat[g], r_t[g] + qg[g][:c, :n]), bd(s_cat[g])) for g in groups]
    y = [sy[g][c:] + qg[g][:c, n:] + av[g][c:2 * c] for g in groups]
    s_new = [sy[g][:c] + qg[g][c:, n:] + av[g][2 * c:] for g in groups]
    return y, s_new


def _rwkv_scan_kernel(r_ref, k_ref, v_ref, lw_ref, kk_ref, a_ref, y_ref, s_ref):
    @pl.when(pl.program_id(1) == 0)
    def _():
        s_ref[...] = jnp.zeros_like(s_ref)

    nb, _, d = r_ref.shape
    where = [(bi, slice(lo, lo + GROUP_LANES)) for bi in range(nb) for lo in range(0, d, GROUP_LANES)]
    load = lambda ref: [ref[bi, :, sl] for bi, sl in where]
    y, s_new = _scan_groups(load(r_ref), load(k_ref), load(v_ref), load(lw_ref), load(kk_ref),
                            load(a_ref), [s_ref[g] for g in range(len(where))], _scan_masks())
    for g, (bi, sl) in enumerate(where):
        y_ref[bi, :, sl] = y[g]
        s_ref[g] = s_new[g]


def _rwkv_scan(r, k, v, lw, kk, a, *, batch_per_step=4):
    b, t, d = r.shape
    nb = batch_per_step
    tile = pl.BlockSpec((nb, CHUNK, d), lambda i, j: (i, j, 0))
    return pl.pallas_call(
        _rwkv_scan_kernel,
        out_shape=jax.ShapeDtypeStruct((b, t, d), F32),
        grid=(b // nb, t // CHUNK),
        in_specs=[tile] * 6,
        out_specs=tile,
        scratch_shapes=[pltpu.VMEM((nb * d // GROUP_LANES, HEAD, GROUP_LANES), F32)],
        compiler_params=pltpu.CompilerParams(
            dimension_semantics=("parallel", "arbitrary"), vmem_limit_bytes=VMEM_LIMIT),
        name="rwkv_scan",
    )(r, k, v, lw, kk, a)


def _rwkv_out_kernel(h_ref, y_ref, bonus_ref, g_ref, gng_ref, gnb_ref, wo_ref, gpost_ref,
                     ones_ref, o_ref):
    ones_bd = ones_ref[...]
    y = y_ref[...]
    mean = _seg_sum(y, ones_bd, pieces=2) * (1.0 / HEAD)
    d = y - mean
    var = _seg_sum(d * d, ones_bd) * (1.0 / HEAD)
    yn = d * lax.rsqrt(var + GN_EPS) * gng_ref[...] + gnb_ref[...] + bonus_ref[...]
    m = _mm(yn * g_ref[...], wo_ref[...])
    o_ref[...] = h_ref[...] + _rms(m, gpost_ref[...])


def _rwkv_out(h, y, bonus, g, gn_g, gn_b, w_o, g_post, ones_bd, *, tm=512):
    m, d = h.shape
    tile = pl.BlockSpec((tm, d), lambda i: (i, 0))
    params = (gn_g, gn_b, w_o, g_post, ones_bd)
    return pl.pallas_call(
        _rwkv_out_kernel,
        out_shape=jax.ShapeDtypeStruct((m, d), F32),
        grid=(m // tm,),
        in_specs=[tile] * 4 + [_full_spec(p) for p in params],
        out_specs=tile,
        compiler_params=pltpu.CompilerParams(
            dimension_semantics=("parallel",), vmem_limit_bytes=VMEM_LIMIT),
        name="rwkv_out",
    )(h, y, bonus, g, *params)


def _kv_proj_kernel(h_ref, g_ref, wk_ref, bk_ref, wvt_ref, bvt_ref, k_ref, vt_ref):
    u = _rms(h_ref[0], g_ref[...]).astype(BF16)
    k_ref[0] = (jnp.dot(u, wk_ref[...], preferred_element_type=F32) + bk_ref[...]).astype(BF16)
    vt_ref[0] = (_mm_nt(wvt_ref[...], u) + bvt_ref[...]).astype(BF16)


def _kv_proj(h3, g, wk, bk, wvt, bvt, *, tm=512):
    b, t, d = h3.shape
    kvw = wk.shape[1]
    params = (g, wk, bk, wvt, bvt)
    return pl.pallas_call(
        _kv_proj_kernel,
        out_shape=[jax.ShapeDtypeStruct((b, t, kvw), BF16), jax.ShapeDtypeStruct((b, kvw, t), BF16)],
        grid=(b, t // tm),
        in_specs=[pl.BlockSpec((1, tm, d), lambda i, j: (i, j, 0))] + [_full_spec(p) for p in params],
        out_specs=[pl.BlockSpec((1, tm, kvw), lambda i, j: (i, j, 0)),
                   pl.BlockSpec((1, kvw, tm), lambda i, j: (i, 0, j))],
        compiler_params=pltpu.CompilerParams(
            dimension_semantics=("parallel", "parallel"), vmem_limit_bytes=VMEM_LIMIT),
        name="kv_proj",
    )(h3, *params)


def _attn_kernel(sinks_ref, h_ref, kp_ref, kc_ref, vtp_ref, vtc_ref, gpre_ref, wqt_ref, bqt_ref,
                 wo_ref, bo_ref, gpost_ref, o_ref):
    step = pl.program_id(1)
    w = WINDOW
    h = h_ref[0]
    u = _rms(h, gpre_ref[...])
    qt = ((_mm_nt(wqt_ref[...], u) + bqt_ref[...]) * (LOG2E * HEAD ** -0.5)).astype(BF16)
    k_cur, k_prev = kc_ref[0], kp_ref[0]
    vt_cur, vt_prev = vtc_ref[0], vtp_ref[0]
    kvw = k_cur.shape[1]
    lanes = GROUP * w

    si = lax.broadcasted_iota(jnp.int32, (2 * w, lanes), 0)
    qi = lax.broadcasted_iota(jnp.int32, (2 * w, lanes), 1) % w
    valid = (si > qi) & (si <= qi + w)
    valid_first = valid & ((si >= w) | (step > 0))
    seg = lax.broadcasted_iota(jnp.int32, (1, lanes), 1) // w

    def scores(n):
        k_band = (jnp.concatenate([k_prev, k_cur[:w]], axis=0) if n == 0
                  else k_cur[(n - 1) * w:(n + 1) * w])
        out = []
        for kh in range(KV_HEADS):
            q_stack = jnp.concatenate(
                [qt[(kh * GROUP + g) * HEAD:(kh * GROUP + g + 1) * HEAD, n * w:(n + 1) * w]
                 for g in range(GROUP)], axis=1)
            pieces = [q_stack]
            if kh > 0:
                pieces.insert(0, jnp.zeros((kh * HEAD, lanes), BF16))
            if kh < KV_HEADS - 1:
                pieces.append(jnp.zeros(((KV_HEADS - 1 - kh) * HEAD, lanes), BF16))
            out.append(jnp.dot(k_band, jnp.concatenate(pieces, axis=0), preferred_element_type=F32))
        return out

    def softmax(n, st):
        mask = valid_first if n == 0 else valid
        out = []
        for kh in range(KV_HEADS):
            sink = sinks_ref[kh * GROUP] * LOG2E
            for g in range(1, GROUP):
                sink = jnp.where(seg == g, sinks_ref[kh * GROUP + g] * LOG2E, sink)
            s = jnp.where(mask, st[kh], MASK_VALUE)
            mx = jnp.maximum(jnp.max(s, axis=0, keepdims=True), sink)
            p = jnp.exp2(s - mx)
            denom = jnp.sum(p, axis=0, keepdims=True) + jnp.exp2(sink - mx)
            out.append((p.astype(BF16), 1.0 / denom))
        return out

    def values(n, pd):
        vt_band = (jnp.concatenate([vt_prev, vt_cur[:, :w]], axis=1) if n == 0
                   else vt_cur[:, (n - 1) * w:(n + 1) * w])
        heads = []
        for kh in range(KV_HEADS):
            p, inv_denom = pd[kh]
            ot = jnp.dot(vt_band[kh * HEAD:(kh + 1) * HEAD], p, preferred_element_type=F32) * inv_denom
            heads += [ot[:, g * w:(g + 1) * w] for g in range(GROUP)]
        return jnp.concatenate(heads, axis=0).astype(BF16)

    st = [scores(0)]
    cols = []
    for n in range(ATTN_BLOCKS):
        if n + 1 < ATTN_BLOCKS:
            st.append(scores(n + 1))
        cols.append(values(n, softmax(n, st[n])))
    ot_all = jnp.concatenate(cols, axis=1)
    m = lax.dot_general(ot_all, wo_ref[...], (((0,), (0,)), ((), ())),
                        preferred_element_type=F32) + bo_ref[...]
    o_ref[0] = h + _rms(m, gpost_ref[...])


def _attn_block(h3, k3, vt3, sinks, g_pre, w_qt, b_qt, w_o, b_o, g_post):
    b, t, d = h3.shape
    kvw = k3.shape[-1]
    tq = ATTN_BLOCKS * WINDOW
    tile = pl.BlockSpec((1, tq, d), lambda i, j: (i, j, 0))
    prev = lambda j: jnp.maximum(j * ATTN_BLOCKS - 1, 0)
    params = (g_pre, w_qt, b_qt, w_o, b_o, g_post)
    return pl.pallas_call(
        _attn_kernel,
        out_shape=jax.ShapeDtypeStruct((b, t, d), F32),
        grid=(b, t // tq),
        in_specs=[
            pl.BlockSpec(memory_space=pltpu.SMEM),
            tile,
            pl.BlockSpec((1, WINDOW, kvw), lambda i, j: (i, prev(j), 0)),
            pl.BlockSpec((1, tq, kvw), lambda i, j: (i, j, 0)),
            pl.BlockSpec((1, kvw, WINDOW), lambda i, j: (i, 0, prev(j))),
            pl.BlockSpec((1, kvw, tq), lambda i, j: (i, 0, j)),
        ] + [_full_spec(p) for p in params],
        out_specs=tile,
        compiler_params=pltpu.CompilerParams(
            dimension_semantics=("parallel", "parallel"), vmem_limit_bytes=VMEM_LIMIT),
        name="swa_block",
    )(sinks, h3, k3, k3, vt3, vt3, *params)


def kernel(x, norm_g, ffn_w_in, ffn_w_out, rwkv_mu, rwkv_w_rkv, rwkv_w_o, rwkv_w0, rwkv_w1, rwkv_w2, rwkv_a0, rwkv_a1, rwkv_a2, rwkv_g1, rwkv_g2, rwkv_k_k, rwkv_k_a, rwkv_r_k, rwkv_gn_g, rwkv_gn_b, kv_norm_g, w_kv, b_kv, attn_w_q, attn_b_q, attn_w_o, attn_b_o, attn_sinks):
    b, t, d = x.shape
    m = b * t
    depth = norm_g.shape[0]
    n_a = rwkv_mu.shape[0]
    row = lambda vec: vec.reshape(1, -1).astype(F32)
    col = lambda vec: vec.reshape(-1, 1).astype(F32)
    bf = lambda w: w.astype(BF16)

    lane_head = jnp.arange(GROUP_LANES) // HEAD
    ones_bd = (lane_head[:, None] == lane_head[None, :]).astype(BF16)

    h = x.reshape(m, d)
    k_sh = vt_sh = None
    w_in, w_out = bf(ffn_w_in), bf(ffn_w_out)
    for layer in range(depth):
        g = norm_g[layer]
        h = _ffn_block(h, row(g[0]), w_in, w_out, row(g[1]), layer, 0)
        if layer < n_a:
            i = layer
            params = (row(g[2]), rwkv_mu[i], bf(rwkv_w_rkv[i]),
                      row(rwkv_w0[i]), bf(rwkv_w1[i]), bf(rwkv_w2[i]),
                      row(rwkv_a0[i]), bf(rwkv_a1[i]), bf(rwkv_a2[i]), bf(rwkv_g1[i]), bf(rwkv_g2[i]),
                      row(rwkv_k_k[i]), row(rwkv_k_a[i]), row(rwkv_r_k[i]), ones_bd)
            r, k, v, lw, kk, a, gate, bonus = _rwkv_prep(h.reshape(b, t, d), params)
            y = _rwkv_scan(r, k, v, lw, kk, a)
            h = _rwkv_out(h, y.reshape(m, d), bonus.reshape(m, d), gate.reshape(m, d),
                          row(rwkv_gn_g[i]), row(rwkv_gn_b[i]), bf(rwkv_w_o[i]), row(g[3]), ones_bd)
        else:
            j = layer - n_a
            h = _attn_block(h.reshape(b, t, d), k_sh, vt_sh, attn_sinks[j].astype(F32), row(g[2]),
                            bf(attn_w_q[j].T), col(attn_b_q[j]), bf(attn_w_o[j]), row(attn_b_o[j]),
                            row(g[3])).reshape(m, d)
        h = _ffn_block(h, row(g[4]), w_in, w_out, row(g[5]), layer, 1)
        if layer == n_a - 1:
            kvw = w_kv.shape[1] // 2
            k_sh, vt_sh = _kv_proj(h.reshape(b, t, d), row(kv_norm_g), bf(w_kv[:, :kvw]), row(b_kv[:kvw]),
                                   bf(w_kv[:, kvw:].T), col(b_kv[kvw:]))
    return h.reshape(b, t, d)
```

```python
import functools

import jax
import jax.numpy as jnp
from jax import lax
from jax.experimental import pallas as pl
from jax.experimental.pallas import tpu as pltpu

F32 = jnp.float32
BF16 = jnp.bfloat16

RMS_EPS = 1e-6
GN_EPS = 64e-5
HEAD = 64
WINDOW = 128
MASK_VALUE = -1e30
KV_HEADS = 4
GROUP = 4
ATTN_BLOCKS = 4
LOG2E = 1.4426950408889634

CHUNK = 64
GROUP_LANES = 256
HEADS_PER_GROUP = GROUP_LANES // HEAD
BASE_BLOCK = 8
PREP_ROWS = 256
FFN_ROWS = 512
EXP_MINUS_HALF = 0.6065306597126334

VMEM_LIMIT = 56 * 1024 * 1024
MXU_WIDTH = 256


def _rms(x, g):
    return x * lax.rsqrt(jnp.mean(x * x, axis=-1, keepdims=True) + RMS_EPS) * g


def _mm(a, b):
    return jnp.dot(a.astype(BF16), b.astype(BF16), preferred_element_type=F32)


def _mm_nt(a, b):
    return lax.dot_general(a.astype(BF16), b.astype(BF16), (((1,), (1,)), ((), ())),
                           preferred_element_type=F32)


def _split2(x):
    hi = x.astype(BF16)
    lo = (x - hi.astype(F32)).astype(BF16)
    return hi, lo


def _split3(x):
    h1 = x.astype(BF16)
    r1 = x - h1.astype(F32)
    h2 = r1.astype(BF16)
    h3 = (r1 - h2.astype(F32)).astype(BF16)
    return h1, h2, h3


def _seg_sum(x, ones_bd, pieces=1):
    out = []
    for lo in range(0, x.shape[1], GROUP_LANES):
        xs = x[:, lo:lo + GROUP_LANES]
        parts = (xs.astype(BF16),) if pieces == 1 else _split2(xs)
        out.append(sum(jnp.dot(p, ones_bd, preferred_element_type=F32) for p in parts))
    return jnp.concatenate(out, axis=1)


def _full_spec(x):
    return pl.BlockSpec(x.shape, lambda *_: (0,) * x.ndim)


def _ffn_chunks(d_ff):
    tiles = d_ff // MXU_WIDTH
    first = (tiles + 1) // 2 * MXU_WIDTH
    return ((0, first), (first, d_ff))


def _ffn_kernel(*refs, has_mix, has_kv):
    refs = list(refs)
    h_ref = refs.pop(0)
    mix_refs = [refs.pop(0) for _ in range(8)] if has_mix else None
    gpre_ref, win_ref, wo_ref, gpost_ref = [refs.pop(0) for _ in range(4)]
    kv_refs = [refs.pop(0) for _ in range(5)] if has_kv else None
    o_ref = refs.pop(0)
    d_ff = wo_ref.shape[0]
    for r0 in range(0, h_ref.shape[0], FFN_ROWS):
        rows = slice(r0, r0 + FFN_ROWS)
        h = h_ref[rows, :]
        if has_mix:
            y_ref, bonus_ref, gate_ref, gng_ref, gnb_ref, wmix_ref, gmix_ref, ones_ref = mix_refs
            ones_bd = ones_ref[...]
            y = y_ref[rows, :]
            dev = y - _seg_sum(y, ones_bd, pieces=2) * (1.0 / HEAD)
            var = _seg_sum(dev * dev, ones_bd) * (1.0 / HEAD)
            yn = dev * lax.rsqrt(var + GN_EPS) * gng_ref[...] + gnb_ref[...] + bonus_ref[rows, :]
            h = h + _rms(_mm(yn * gate_ref[rows, :], wmix_ref[...]), gmix_ref[...])
        xn = _rms(h, gpre_ref[...]).astype(BF16)
        acc = None
        for lo, hi in _ffn_chunks(d_ff):
            gate = jnp.dot(xn, win_ref[:, lo:hi], preferred_element_type=F32)
            up = jnp.dot(xn, win_ref[:, d_ff + lo:d_ff + hi], preferred_element_type=F32)
            act = (gate * jax.nn.sigmoid(gate) * up).astype(BF16)
            part = jnp.dot(act, wo_ref[lo:hi, :], preferred_element_type=F32)
            acc = part if acc is None else acc + part
        h = h + 0.5 * _rms(acc, gpost_ref[...])
        o_ref[rows, :] = h
        if has_kv:
            gkv_ref, wk_ref, bk_ref, wvt_ref, bvt_ref = kv_refs
            k_ref, vt_ref = refs
            u = _rms(h, gkv_ref[...]).astype(BF16)
            k_ref[rows, :] = (jnp.dot(u, wk_ref[...], preferred_element_type=F32) + bk_ref[...]).astype(BF16)
            vt_ref[0, :, rows] = (_mm_nt(wvt_ref[...], u) + bvt_ref[...]).astype(BF16)


def _ffn_block(h, g_pre, w_in, w_out, g_post, layer, which, *, mix=None, kv=None, seq_len=None):
    m, d = h.shape
    tm = FFN_ROWS if mix is not None else 2 * FFN_ROWS
    tile = pl.BlockSpec((tm, d), lambda i: (i, 0))
    resident = lambda x: pl.BlockSpec(x.shape, lambda i: (0,) * x.ndim, pipeline_mode=pl.Buffered(1))
    picked = lambda x: pl.BlockSpec((None, None) + x.shape[2:], lambda i: (layer, which, 0, 0),
                                    pipeline_mode=pl.Buffered(1))
    args, in_specs = [h], [tile]
    if mix is not None:
        args += list(mix)
        in_specs += [tile] * 3 + [resident(p) for p in mix[3:]]
    args += [g_pre, w_in, w_out, g_post]
    in_specs += [resident(g_pre), picked(w_in), picked(w_out), resident(g_post)]
    out_shape, out_specs = [jax.ShapeDtypeStruct((m, d), F32)], [tile]
    if kv is not None:
        args += list(kv)
        in_specs += [resident(p) for p in kv]
        kvw = kv[1].shape[1]
        steps = seq_len // tm
        out_shape += [jax.ShapeDtypeStruct((m, kvw), BF16),
                      jax.ShapeDtypeStruct((m // seq_len, kvw, seq_len), BF16)]
        out_specs += [pl.BlockSpec((tm, kvw), lambda i: (i, 0)),
                      pl.BlockSpec((1, kvw, tm), lambda i: (i // steps, 0, i % steps))]
    out = pl.pallas_call(
        functools.partial(_ffn_kernel, has_mix=mix is not None, has_kv=kv is not None),
        out_shape=out_shape,
        grid=(m // tm,),
        in_specs=in_specs,
        out_specs=out_specs,
        compiler_params=pltpu.CompilerParams(
            dimension_semantics=("parallel",), vmem_limit_bytes=VMEM_LIMIT),
        name="ffn_block",
    )(*args)
    return out if kv is not None else out[0]


def _rwkv_prep_kernel(h_ref, g_ref, mu_ref, wrkv_ref, w0_ref, w1_ref, w2_ref,
                      a0_ref, a1_ref, a2_ref, g1_ref, g2_ref, kk_ref, ka_ref, rk_ref,
                      ones_ref,
                      r_out, k_out, v_out, lw_out, kk_out, a_out, g_out, bonus_out,
                      carry_ref):
    @pl.when(pl.program_id(1) == 0)
    def _():
        carry_ref[...] = jnp.zeros_like(carry_ref)

    ones_bd = ones_ref[...]
    mu = mu_ref[...]
    tm = h_ref.shape[1]
    row = lax.broadcasted_iota(jnp.int32, (PREP_ROWS, h_ref.shape[2]), 0)
    last = carry_ref[0:1, :]
    for r0 in range(0, tm, PREP_ROWS):
        rows = slice(r0, r0 + PREP_ROWS)
        u = _rms(h_ref[0, rows, :], g_ref[...])
        prev = jnp.where(row == 0, last, pltpu.roll(u, shift=1, axis=0))
        last = u[PREP_ROWS - 1:PREP_ROWS, :]
        xx = prev - u
        xr, xw, xk, xv, xa, xg = [u + xx * mu[i:i + 1, :] for i in range(6)]

        r = _mm(xr, wrkv_ref[0])
        k = _mm(xk, wrkv_ref[1])
        v = _mm(xv, wrkv_ref[2])
        x = w0_ref[...] + _mm(jnp.tanh(_mm(xw, w1_ref[...])), w2_ref[...])
        lw = jax.nn.sigmoid(x) * (-EXP_MINUS_HALF)
        a = jax.nn.sigmoid(a0_ref[...] + _mm(_mm(xa, a1_ref[...]), a2_ref[...]))
        g = _mm(jax.nn.sigmoid(_mm(xg, g1_ref[...])), g2_ref[...])

        kk = k * kk_ref[...]
        norm = jnp.sqrt(_seg_sum(kk * kk, ones_bd))
        kk = kk / jnp.maximum(norm, 1e-12)
        k = k * (1.0 + (a - 1.0) * ka_ref[...])

        r_out[0, rows, :] = r
        k_out[0, rows, :] = k
        v_out[0, rows, :] = v.astype(BF16)
        lw_out[0, rows, :] = lw
        kk_out[0, rows, :] = kk
        a_out[0, rows, :] = a
        g_out[0, rows, :] = g.astype(BF16)
        bonus_out[0, rows, :] = (_seg_sum(r * k * rk_ref[...], ones_bd) * v).astype(BF16)
    carry_ref[0:1, :] = last


def _rwkv_prep(h3, params, *, tm=512):
    b, t, d = h3.shape
    tile = pl.BlockSpec((1, tm, d), lambda i, j: (i, j, 0))
    resident = lambda x: pl.BlockSpec(x.shape, lambda i, j: (0,) * x.ndim, pipeline_mode=pl.Buffered(1))
    dtypes = [F32, F32, BF16, F32, F32, F32, BF16, BF16]
    return pl.pallas_call(
        _rwkv_prep_kernel,
        out_shape=[jax.ShapeDtypeStruct((b, t, d), dt) for dt in dtypes],
        grid=(b, t // tm),
        in_specs=[tile] + [resident(p) for p in params],
        out_specs=[tile] * 8,
        scratch_shapes=[pltpu.VMEM((8, d), F32)],
        compiler_params=pltpu.CompilerParams(
            dimension_semantics=("parallel", "arbitrary"), vmem_limit_bytes=VMEM_LIMIT),
        name="rwkv_prep",
    )(h3, *params)


def _scan_masks():
    c, n = CHUNK, GROUP_LANES
    row_s = lax.broadcasted_iota(jnp.int32, (c, n), 0)
    col_s = lax.broadcasted_iota(jnp.int32, (c, n), 1) % c
    strict = row_s > col_s
    incl = row_s >= col_s
    eye = row_s == col_s
    base = strict & ((row_s // BASE_BLOCK) == (col_s // BASE_BLOCK))
    offs = []
    b = BASE_BLOCK
    while b < c:
        offs.append(((row_s // (2 * b)) == (col_s // (2 * b)))
                    & ((row_s // b) % 2 == 1) & ((col_s // b) % 2 == 0))
        b *= 2
    row_b = lax.broadcasted_iota(jnp.int32, (n, n), 0)
    col_b = lax.broadcasted_iota(jnp.int32, (n, n), 1)
    mask_bd = (row_b // HEAD) == (col_b // HEAD)
    tri_ones = jnp.where(lax.broadcasted_iota(jnp.int32, (c, c), 0)
                         >= lax.broadcasted_iota(jnp.int32, (c, c), 1), 1.0, 0.0).astype(BF16)
    return mask_bd, strict, incl, eye, base, tuple(offs), tri_ones


def _block_diag(z, mask_bd):
    tiled = jnp.concatenate([z] * HEADS_PER_GROUP, axis=0)
    return jnp.where(mask_bd, tiled, 0.0)


def _head_transpose(x):
    xt = x.T
    return jnp.concatenate([xt[h * HEAD:(h + 1) * HEAD, :] for h in range(HEADS_PER_GROUP)], axis=1)


def _scan_groups(r, k, v, lw, kk, a, s_cat, masks):
    mask_bd, strict, incl, eye, base, offs, tri_ones = masks
    c, n = CHUNK, GROUP_LANES
    groups = range(len(r))
    bd = lambda z: _block_diag(z, mask_bd)
    cat0 = lambda *xs: jnp.concatenate(xs, axis=0)
    cat1 = lambda *xs: jnp.concatenate(xs, axis=1)

    l_cum = [sum(jnp.dot(tri_ones, piece, preferred_element_type=F32) for piece in _split2(lw[g]))
             for g in groups]
    e_l = [jnp.exp(l_cum[g]) for g in groups]
    e_nl = [jnp.exp(-l_cum[g]) for g in groups]
    a_t = [-kk[g] * jnp.exp(l_cum[g] - lw[g]) for g in groups]
    r_t = [r[g] * e_l[g] for g in groups]
    b_t = [kk[g] * a[g] * e_nl[g] for g in groups]
    k_t = [k[g] * e_nl[g] for g in groups]
    p_end = [e_l[g][c - 1:c, :] for g in groups]
    b_ht = [_head_transpose(b_t[g] * p_end[g]) for g in groups]
    k_ht = [_head_transpose(k_t[g] * p_end[g]) for g in groups]

    a_all = [_mm_nt(cat0(a_t[g], r_t[g]), cat0(bd(b_t[g]), bd(k_t[g]))) for g in groups]
    a_ab = [a_all[g][:c, :n] for g in groups]

    p = [jnp.where(base, a_ab[g], 0.0) for g in groups]
    inv = [jnp.where(eye, 1.0, 0.0) + p[g] for g in groups]
    p = [_mm(p[g], bd(p[g])) for g in groups]
    for _ in range(BASE_BLOCK.bit_length() - 3):
        both = [_mm(cat0(p[g], inv[g]), bd(p[g])) for g in groups]
        p = [both[g][:c] for g in groups]
        inv = [inv[g] + both[g][c:] for g in groups]
    inv = [inv[g] + _mm(inv[g], bd(p[g])) for g in groups]
    for off in offs:
        x = [_mm(jnp.where(off, a_ab[g], 0.0), bd(inv[g])) for g in groups]
        inv = [inv[g] + _mm(inv[g], bd(x[g])) for g in groups]

    av = [_mm(cat0(jnp.where(strict, a_all[g][:c, n:], 0.0), jnp.where(incl, a_all[g][c:, n:], 0.0),
                   k_ht[g]), bd(v[g])) for g in groups]
    wz = [_mm(inv[g], cat1(bd(a_t[g]), bd(av[g][:c]))) for g in groups]
    qg = [_mm(cat0(jnp.where(incl, a_all[g][c:, :n], 0.0), b_ht[g]),
              cat1(bd(wz[g][:, :n]), bd(wz[g][:, n:]))) for g in groups]
    g_cat = [qg[g][c:, :n] + jnp.where(eye, p_end[g], 0.0) for g in groups]
    sy = [_mm(cat0(g_cat[g], r_t[g] + qg[g][:c, :n]), bd(s_cat[g])) for g in groups]
    y = [sy[g][c:] + qg[g][:c, n:] + av[g][c:2 * c] for g in groups]
    s_new = [sy[g][:c] + qg[g][c:, n:] + av[g][2 * c:] for g in groups]
    return y, s_new


def _rwkv_scan_kernel(r_ref, k_ref, v_ref, lw_ref, kk_ref, a_ref, y_ref, s_ref):
    @pl.when(pl.program_id(1) == 0)
    def _():
        s_ref[...] = jnp.zeros_like(s_ref)

    nb, _, d = r_ref.shape
    where = [(bi, slice(lo, lo + GROUP_LANES)) for bi in range(nb) for lo in range(0, d, GROUP_LANES)]
    load = lambda ref: [ref[bi, :, sl] for bi, sl in where]
    y, s_new = _scan_groups(load(r_ref), load(k_ref), load(v_ref), load(lw_ref), load(kk_ref),
                            load(a_ref), [s_ref[g] for g in range(len(where))], _scan_masks())
    for g, (bi, sl) in enumerate(where):
        y_ref[bi, :, sl] = y[g]
        s_ref[g] = s_new[g]


def _rwkv_scan(r, k, v, lw, kk, a, *, batch_per_step=4):
    b, t, d = r.shape
    nb = batch_per_step
    tile = pl.BlockSpec((nb, CHUNK, d), lambda i, j: (i, j, 0))
    return pl.pallas_call(
        _rwkv_scan_kernel,
        out_shape=jax.ShapeDtypeStruct((b, t, d), F32),
        grid=(b // nb, t // CHUNK),
        in_specs=[tile] * 6,
        out_specs=tile,
        scratch_shapes=[pltpu.VMEM((nb * d // GROUP_LANES, HEAD, GROUP_LANES), F32)],
        compiler_params=pltpu.CompilerParams(
            dimension_semantics=("parallel", "arbitrary"), vmem_limit_bytes=VMEM_LIMIT),
        name="rwkv_scan",
    )(r, k, v, lw, kk, a)


def _attn_kernel(sinks_ref, h_ref, kp_ref, kc_ref, vtp_ref, vtc_ref, gpre_ref, wqt_ref, bqt_ref,
                 wo_ref, bo_ref, gpost_ref, o_ref):
    step = pl.program_id(1)
    w = WINDOW
    h = h_ref[0]
    u = _rms(h, gpre_ref[...])
    qt = ((_mm_nt(wqt_ref[...], u) + bqt_ref[...]) * (LOG2E * HEAD ** -0.5)).astype(BF16)
    k_cur, k_prev = kc_ref[0], kp_ref[0]
    vt_cur, vt_prev = vtc_ref[0], vtp_ref[0]
    kvw = k_cur.shape[1]
    lanes = GROUP * w

    si = lax.broadcasted_iota(jnp.int32, (2 * w, lanes), 0)
    qi = lax.broadcasted_iota(jnp.int32, (2 * w, lanes), 1) % w
    valid = (si > qi) & (si <= qi + w)
    valid_first = valid & ((si >= w) | (step > 0))
    seg = lax.broadcasted_iota(jnp.int32, (1, lanes), 1) // w

    def scores(n):
        k_band = (jnp.concatenate([k_prev, k_cur[:w]], axis=0) if n == 0
                  else k_cur[(n - 1) * w:(n + 1) * w])
        out = []
        for kh in range(KV_HEADS):
            q_stack = jnp.concatenate(
                [qt[(kh * GROUP + g) * HEAD:(kh * GROUP + g + 1) * HEAD, n * w:(n + 1) * w]
                 for g in range(GROUP)], axis=1)
            pieces = [q_stack]
            if kh > 0:
                pieces.insert(0, jnp.zeros((kh * HEAD, lanes), BF16))
            if kh < KV_HEADS - 1:
                pieces.append(jnp.zeros(((KV_HEADS - 1 - kh) * HEAD, lanes), BF16))
            out.append(jnp.dot(k_band, jnp.concatenate(pieces, axis=0), preferred_element_type=F32))
        return out

    def softmax(n, st):
        mask = valid_first if n == 0 else valid
        out = []
        for kh in range(KV_HEADS):
            sink = sinks_ref[kh * GROUP] * LOG2E
            for g in range(1, GROUP):
                sink = jnp.where(seg == g, sinks_ref[kh * GROUP + g] * LOG2E, sink)
            s = jnp.where(mask, st[kh], MASK_VALUE)
            mx = jnp.maximum(jnp.max(s, axis=0, keepdims=True), sink)
            p = jnp.exp2(s - mx)
            denom = jnp.sum(p, axis=0, keepdims=True) + jnp.exp2(sink - mx)
            out.append((p.astype(BF16), 1.0 / denom))
        return out

    def values(n, pd):
        vt_band = (jnp.concatenate([vt_prev, vt_cur[:, :w]], axis=1) if n == 0
                   else vt_cur[:, (n - 1) * w:(n + 1) * w])
        heads = []
        for kh in range(KV_HEADS):
            p, inv_denom = pd[kh]
            ot = jnp.dot(vt_band[kh * HEAD:(kh + 1) * HEAD], p, preferred_element_type=F32) * inv_denom
            heads += [ot[:, g * w:(g + 1) * w] for g in range(GROUP)]
        return jnp.concatenate(heads, axis=0).astype(BF16)

    st = [scores(0)]
    cols = []
    for n in range(ATTN_BLOCKS):
        if n + 1 < ATTN_BLOCKS:
            st.append(scores(n + 1))
        cols.append(values(n, softmax(n, st[n])))
    ot_all = jnp.concatenate(cols, axis=1)
    m = lax.dot_general(ot_all, wo_ref[...], (((0,), (0,)), ((), ())),
                        preferred_element_type=F32) + bo_ref[...]
    o_ref[0] = h + _rms(m, gpost_ref[...])


def _attn_block(h3, k3, vt3, sinks, g_pre, w_qt, b_qt, w_o, b_o, g_post):
    b, t, d = h3.shape
    kvw = k3.shape[-1]
    tq = ATTN_BLOCKS * WINDOW
    tile = pl.BlockSpec((1, tq, d), lambda i, j: (i, j, 0))
    prev = lambda j: jnp.maximum(j * ATTN_BLOCKS - 1, 0)
    params = (g_pre, w_qt, b_qt, w_o, b_o, g_post)
    return pl.pallas_call(
        _attn_kernel,
        out_shape=jax.ShapeDtypeStruct((b, t, d), F32),
        grid=(b, t // tq),
        in_specs=[
            pl.BlockSpec(memory_space=pltpu.SMEM),
            tile,
            pl.BlockSpec((1, WINDOW, kvw), lambda i, j: (i, prev(j), 0)),
            pl.BlockSpec((1, tq, kvw), lambda i, j: (i, j, 0)),
            pl.BlockSpec((1, kvw, WINDOW), lambda i, j: (i, 0, prev(j))),
            pl.BlockSpec((1, kvw, tq), lambda i, j: (i, 0, j)),
        ] + [_full_spec(p) for p in params],
        out_specs=tile,
        compiler_params=pltpu.CompilerParams(
            dimension_semantics=("parallel", "parallel"), vmem_limit_bytes=VMEM_LIMIT),
        name="swa_block",
    )(sinks, h3, k3, k3, vt3, vt3, *params)


def kernel(x, norm_g, ffn_w_in, ffn_w_out, rwkv_mu, rwkv_w_rkv, rwkv_w_o, rwkv_w0, rwkv_w1, rwkv_w2, rwkv_a0, rwkv_a1, rwkv_a2, rwkv_g1, rwkv_g2, rwkv_k_k, rwkv_k_a, rwkv_r_k, rwkv_gn_g, rwkv_gn_b, kv_norm_g, w_kv, b_kv, attn_w_q, attn_b_q, attn_w_o, attn_b_o, attn_sinks):
    b, t, d = x.shape
    m = b * t
    depth = norm_g.shape[0]
    n_a = rwkv_mu.shape[0]
    row = lambda vec: vec.reshape(1, -1).astype(F32)
    col = lambda vec: vec.reshape(-1, 1).astype(F32)
    bf = lambda w: w.astype(BF16)

    lane_head = jnp.arange(GROUP_LANES) // HEAD
    ones_bd = (lane_head[:, None] == lane_head[None, :]).astype(BF16)

    h = x.reshape(m, d)
    k_sh = vt_sh = None
    w_in, w_out = bf(ffn_w_in), bf(ffn_w_out)
    kvw = w_kv.shape[1] // 2
    kv_params = (row(kv_norm_g), bf(w_kv[:, :kvw]), row(b_kv[:kvw]), bf(w_kv[:, kvw:].T), col(b_kv[kvw:]))
    for layer in range(depth):
        g = norm_g[layer]
        h = _ffn_block(h, row(g[0]), w_in, w_out, row(g[1]), layer, 0)
        mix = None
        if layer < n_a:
            i = layer
            params = (row(g[2]), rwkv_mu[i], bf(rwkv_w_rkv[i]),
                      row(rwkv_w0[i]), bf(rwkv_w1[i]), bf(rwkv_w2[i]),
                      row(rwkv_a0[i]), bf(rwkv_a1[i]), bf(rwkv_a2[i]), bf(rwkv_g1[i]), bf(rwkv_g2[i]),
                      row(rwkv_k_k[i]), row(rwkv_k_a[i]), row(rwkv_r_k[i]), ones_bd)
            r, k, v, lw, kk, a, gate, bonus = _rwkv_prep(h.reshape(b, t, d), params)
            y = _rwkv_scan(r, k, v, lw, kk, a)
            mix = (y.reshape(m, d), bonus.reshape(m, d), gate.reshape(m, d), row(rwkv_gn_g[i]),
                   row(rwkv_gn_b[i]), bf(rwkv_w_o[i]), row(g[3]), ones_bd)
        else:
            j = layer - n_a
            h = _attn_block(h.reshape(b, t, d), k_sh, vt_sh, attn_sinks[j].astype(F32), row(g[2]),
                            bf(attn_w_q[j].T), col(attn_b_q[j]), bf(attn_w_o[j]), row(attn_b_o[j]),
                            row(g[3])).reshape(m, d)
        if layer == n_a - 1:
            h, k_sh, vt_sh = _ffn_block(h, row(g[4]), w_in, w_out, row(g[5]), layer, 1, mix=mix,
                                        kv=kv_params, seq_len=t)
            k_sh = k_sh.reshape(b, t, kvw)
        else:
            h = _ffn_block(h, row(g[4]), w_in, w_out, row(g[5]), layer, 1, mix=mix)
    return h.reshape(b, t, d)
```

```python
import functools

import jax
import jax.numpy as jnp
from jax import lax
from jax.experimental import pallas as pl
from jax.experimental.pallas import tpu as pltpu

F32 = jnp.float32
BF16 = jnp.bfloat16

RMS_EPS = 1e-6
GN_EPS = 64e-5
HEAD = 64
WINDOW = 128
MASK_VALUE = -1e30
KV_HEADS = 4
GROUP = 4
ATTN_BLOCKS = 4
LOG2E = 1.4426950408889634

CHUNK = 64
GROUP_LANES = 256
HEADS_PER_GROUP = GROUP_LANES // HEAD
BASE_BLOCK = 8
PREP_ROWS = 256
FFN_ROWS = 512
FFN_WEIGHT_STEPS = 11
EXP_MINUS_HALF = 0.6065306597126334

VMEM_LIMIT = 56 * 1024 * 1024
MXU_WIDTH = 256


def _rms(x, g):
    return x * lax.rsqrt(jnp.mean(x * x, axis=-1, keepdims=True) + RMS_EPS) * g


def _mm(a, b):
    return jnp.dot(a.astype(BF16), b.astype(BF16), preferred_element_type=F32)


def _mm_nt(a, b):
    return lax.dot_general(a.astype(BF16), b.astype(BF16), (((1,), (1,)), ((), ())),
                           preferred_element_type=F32)


def _split2(x):
    hi = x.astype(BF16)
    lo = (x - hi.astype(F32)).astype(BF16)
    return hi, lo


def _split3(x):
    h1 = x.astype(BF16)
    r1 = x - h1.astype(F32)
    h2 = r1.astype(BF16)
    h3 = (r1 - h2.astype(F32)).astype(BF16)
    return h1, h2, h3


def _seg_sum(x, ones_bd, pieces=1):
    out = []
    for lo in range(0, x.shape[1], GROUP_LANES):
        xs = x[:, lo:lo + GROUP_LANES]
        parts = (xs.astype(BF16),) if pieces == 1 else _split2(xs)
        out.append(sum(jnp.dot(p, ones_bd, preferred_element_type=F32) for p in parts))
    return jnp.concatenate(out, axis=1)


def _full_spec(x):
    return pl.BlockSpec(x.shape, lambda *_: (0,) * x.ndim)


def _ffn_chunks(d_ff):
    tiles = d_ff // MXU_WIDTH
    first = (tiles + 1) // 2 * MXU_WIDTH
    return ((0, first), (first, d_ff))


def _ffn_kernel(*refs, has_mix, has_kv):
    refs = list(refs)
    h_ref = refs.pop(0)
    mix_refs = [refs.pop(0) for _ in range(8)] if has_mix else None
    gpre_ref, win_chunk_ref, wo_chunk_ref, gpost_ref = [refs.pop(0) for _ in range(4)]
    kv_refs = [refs.pop(0) for _ in range(5)] if has_kv else None
    o_ref = refs.pop(0)
    win_ref, wo_ref = refs[-2:]
    d_ff = wo_ref.shape[0]
    step = pl.program_id(0)

    cw, cr = win_chunk_ref.shape[1], wo_chunk_ref.shape[0]
    for c in range(FFN_WEIGHT_STEPS):
        @pl.when(step == c)
        def _(c=c):
            win_ref[:, c * cw:(c + 1) * cw] = win_chunk_ref[...].astype(BF16)
            wo_ref[c * cr:(c + 1) * cr, :] = wo_chunk_ref[...].astype(BF16)

    @pl.when(step >= FFN_WEIGHT_STEPS)
    def _():
        _ffn_rows(h_ref, mix_refs, gpre_ref, win_ref, wo_ref, gpost_ref, kv_refs, o_ref,
                  refs[:-2], d_ff)


def _ffn_rows(h_ref, mix_refs, gpre_ref, win_ref, wo_ref, gpost_ref, kv_refs, o_ref, kv_out_refs, d_ff):
    has_mix, has_kv = mix_refs is not None, kv_refs is not None
    for r0 in range(0, h_ref.shape[0], FFN_ROWS):
        rows = slice(r0, r0 + FFN_ROWS)
        h = h_ref[rows, :]
        if has_mix:
            y_ref, bonus_ref, gate_ref, gng_ref, gnb_ref, wmix_ref, gmix_ref, ones_ref = mix_refs
            ones_bd = ones_ref[...]
            y = y_ref[rows, :]
            dev = y - _seg_sum(y, ones_bd, pieces=2) * (1.0 / HEAD)
            var = _seg_sum(dev * dev, ones_bd) * (1.0 / HEAD)
            yn = dev * lax.rsqrt(var + GN_EPS) * gng_ref[...] + gnb_ref[...] + bonus_ref[rows, :]
            h = h + _rms(_mm(yn * gate_ref[rows, :], wmix_ref[...]), gmix_ref[...])
        xn = _rms(h, gpre_ref[...]).astype(BF16)
        acc = None
        for lo, hi in _ffn_chunks(d_ff):
            gate = jnp.dot(xn, win_ref[:, lo:hi], preferred_element_type=F32)
            up = jnp.dot(xn, win_ref[:, d_ff + lo:d_ff + hi], preferred_element_type=F32)
            act = (gate * jax.nn.sigmoid(gate) * up).astype(BF16)
            part = jnp.dot(act, wo_ref[lo:hi, :], preferred_element_type=F32)
            acc = part if acc is None else acc + part
        h = h + 0.5 * _rms(acc, gpost_ref[...])
        o_ref[rows, :] = h
        if has_kv:
            gkv_ref, wk_ref, bk_ref, wvt_ref, bvt_ref = kv_refs
            k_ref, vt_ref = kv_out_refs
            u = _rms(h, gkv_ref[...]).astype(BF16)
            k_ref[rows, :] = (jnp.dot(u, wk_ref[...], preferred_element_type=F32) + bk_ref[...]).astype(BF16)
            vt_ref[0, :, rows] = (_mm_nt(wvt_ref[...], u) + bvt_ref[...]).astype(BF16)


def _ffn_block(h, g_pre, w_in, w_out, g_post, layer, which, *, mix=None, kv=None, seq_len=None):
    m, d = h.shape
    d_ff = w_out.shape[2]
    nw = FFN_WEIGHT_STEPS
    tm = FFN_ROWS if mix is not None else 2 * FFN_ROWS
    row_tile = lambda i: jnp.maximum(i - nw, 0)
    chunk = lambda i: jnp.minimum(i, nw - 1)
    tile = pl.BlockSpec((tm, d), lambda i: (row_tile(i), 0))
    resident = lambda x: pl.BlockSpec(x.shape, lambda i: (0,) * x.ndim, pipeline_mode=pl.Buffered(1))
    args, in_specs = [h], [tile]
    if mix is not None:
        args += list(mix)
        in_specs += [tile] * 3 + [resident(p) for p in mix[3:]]
    args += [g_pre, w_in, w_out, g_post]
    in_specs += [resident(g_pre),
                 pl.BlockSpec((None, None, d, 2 * d_ff // nw), lambda i: (layer, which, 0, chunk(i))),
                 pl.BlockSpec((None, None, d_ff // nw, d), lambda i: (layer, which, chunk(i), 0)),
                 resident(g_post)]
    out_shape, out_specs = [jax.ShapeDtypeStruct((m, d), F32)], [tile]
    if kv is not None:
        args += list(kv)
        in_specs += [resident(p) for p in kv]
        kvw = kv[1].shape[1]
        steps = seq_len // tm
        out_shape += [jax.ShapeDtypeStruct((m, kvw), BF16),
                      jax.ShapeDtypeStruct((m // seq_len, kvw, seq_len), BF16)]
        out_specs += [pl.BlockSpec((tm, kvw), lambda i: (row_tile(i), 0)),
                      pl.BlockSpec((1, kvw, tm), lambda i: (row_tile(i) // steps, 0, row_tile(i) % steps))]
    out = pl.pallas_call(
        functools.partial(_ffn_kernel, has_mix=mix is not None, has_kv=kv is not None),
        out_shape=out_shape,
        grid=(nw + m // tm,),
        in_specs=in_specs,
        out_specs=out_specs,
        scratch_shapes=[pltpu.VMEM((d, 2 * d_ff), BF16), pltpu.VMEM((d_ff, d), BF16)],
        compiler_params=pltpu.CompilerParams(
            dimension_semantics=("arbitrary",), vmem_limit_bytes=VMEM_LIMIT),
        name="ffn_block",
    )(*args)
    return out if kv is not None else out[0]


def _rwkv_prep_kernel(h_ref, g_ref, mu_ref, wrkv_ref, w0_ref, w1_ref, w2_ref,
                      a0_ref, a1_ref, a2_ref, g1_ref, g2_ref, kk_ref, ka_ref, rk_ref,
                      ones_ref,
                      r_out, k_out, v_out, lw_out, kk_out, a_out, g_out, bonus_out,
                      carry_ref):
    @pl.when(pl.program_id(1) == 0)
    def _():
        carry_ref[...] = jnp.zeros_like(carry_ref)

    ones_bd = ones_ref[...]
    mu = mu_ref[...]
    tm = h_ref.shape[1]
    row = lax.broadcasted_iota(jnp.int32, (PREP_ROWS, h_ref.shape[2]), 0)
    last = carry_ref[0:1, :]
    for r0 in range(0, tm, PREP_ROWS):
        rows = slice(r0, r0 + PREP_ROWS)
        u = _rms(h_ref[0, rows, :], g_ref[...])
        prev = jnp.where(row == 0, last, pltpu.roll(u, shift=1, axis=0))
        last = u[PREP_ROWS - 1:PREP_ROWS, :]
        xx = prev - u
        xr, xw, xk, xv, xa, xg = [u + xx * mu[i:i + 1, :] for i in range(6)]

        r = _mm(xr, wrkv_ref[0])
        k = _mm(xk, wrkv_ref[1])
        v = _mm(xv, wrkv_ref[2])
        x = w0_ref[...] + _mm(jnp.tanh(_mm(xw, w1_ref[...])), w2_ref[...])
        lw = jax.nn.sigmoid(x) * (-EXP_MINUS_HALF)
        a = jax.nn.sigmoid(a0_ref[...] + _mm(_mm(xa, a1_ref[...]), a2_ref[...]))
        g = _mm(jax.nn.sigmoid(_mm(xg, g1_ref[...])), g2_ref[...])

        kk = k * kk_ref[...]
        norm = jnp.sqrt(_seg_sum(kk * kk, ones_bd))
        kk = kk / jnp.maximum(norm, 1e-12)
        k = k * (1.0 + (a - 1.0) * ka_ref[...])

        r_out[0, rows, :] = r
        k_out[0, rows, :] = k
        v_out[0, rows, :] = v.astype(BF16)
        lw_out[0, rows, :] = lw
        kk_out[0, rows, :] = kk
        a_out[0, rows, :] = a
        g_out[0, rows, :] = g.astype(BF16)
        bonus_out[0, rows, :] = (_seg_sum(r * k * rk_ref[...], ones_bd) * v).astype(BF16)
    carry_ref[0:1, :] = last


def _rwkv_prep(h3, params, *, tm=512):
    b, t, d = h3.shape
    tile = pl.BlockSpec((1, tm, d), lambda i, j: (i, j, 0))
    resident = lambda x: pl.BlockSpec(x.shape, lambda i, j: (0,) * x.ndim, pipeline_mode=pl.Buffered(1))
    dtypes = [F32, F32, BF16, F32, F32, F32, BF16, BF16]
    return pl.pallas_call(
        _rwkv_prep_kernel,
        out_shape=[jax.ShapeDtypeStruct((b, t, d), dt) for dt in dtypes],
        grid=(b, t // tm),
        in_specs=[tile] + [resident(p) for p in params],
        out_specs=[tile] * 8,
        scratch_shapes=[pltpu.VMEM((8, d), F32)],
        compiler_params=pltpu.CompilerParams(
            dimension_semantics=("parallel", "arbitrary"), vmem_limit_bytes=VMEM_LIMIT),
        name="rwkv_prep",
    )(h3, *params)


def _scan_masks():
    c, n = CHUNK, GROUP_LANES
    row_s = lax.broadcasted_iota(jnp.int32, (c, n), 0)
    col_s = lax.broadcasted_iota(jnp.int32, (c, n), 1) % c
    strict = row_s > col_s
    incl = row_s >= col_s
    eye = row_s == col_s
    base = strict & ((row_s // BASE_BLOCK) == (col_s // BASE_BLOCK))
    offs = []
    b = BASE_BLOCK
    while b < c:
        offs.append(((row_s // (2 * b)) == (col_s // (2 * b)))
                    & ((row_s // b) % 2 == 1) & ((col_s // b) % 2 == 0))
        b *= 2
    row_b = lax.broadcasted_iota(jnp.int32, (n, n), 0)
    col_b = lax.broadcasted_iota(jnp.int32, (n, n), 1)
    mask_bd = (row_b // HEAD) == (col_b // HEAD)
    row_ge = {}
    shift = 1
    while shift < c:
        row_ge[shift] = row_s >= shift
        shift *= 2
    return mask_bd, strict, incl, eye, base, tuple(offs), row_ge


def _block_diag(z, mask_bd):
    tiled = jnp.concatenate([z] * HEADS_PER_GROUP, axis=0)
    return jnp.where(mask_bd, tiled, 0.0)


def _head_transpose(x):
    xt = x.T
    return jnp.concatenate([xt[h * HEAD:(h + 1) * HEAD, :] for h in range(HEADS_PER_GROUP)], axis=1)


def _scan_groups(r, k, v, lw, kk, a, s_cat, masks):
    mask_bd, strict, incl, eye, base, offs, row_ge = masks
    c, n = CHUNK, GROUP_LANES
    groups = range(len(r))
    bd = lambda z: _block_diag(z, mask_bd)
    cat0 = lambda *xs: jnp.concatenate(xs, axis=0)

    l_cum = list(lw)
    shift = 1
    while shift < c:
        l_cum = [l_cum[g] + jnp.where(row_ge[shift], pltpu.roll(l_cum[g], shift=shift, axis=0), 0.0)
                 for g in groups]
        shift *= 2
    e_l = [jnp.exp(l_cum[g]) for g in groups]
    e_nl = [jnp.exp(-l_cum[g]) for g in groups]
    a_t = [-kk[g] * jnp.exp(l_cum[g] - lw[g]) for g in groups]
    r_t = [r[g] * e_l[g] for g in groups]
    b_t = [kk[g] * a[g] * e_nl[g] for g in groups]
    k_t = [k[g] * e_nl[g] for g in groups]
    p_end = [e_l[g][c - 1:c, :] for g in groups]
    b_ht = [_head_transpose(b_t[g] * p_end[g]) for g in groups]
    k_ht = [_head_transpose(k_t[g] * p_end[g]) for g in groups]

    a_all = [_mm_nt(cat0(a_t[g], r_t[g]), cat0(bd(b_t[g]), bd(k_t[g]))) for g in groups]
    a_ab = [a_all[g][:c, :n] for g in groups]
    xs = [_mm(cat0(a_t[g], r_t[g], jnp.where(eye, p_end[g], 0.0)), bd(s_cat[g])) for g in groups]
    av = [_mm(cat0(jnp.where(strict, a_all[g][:c, n:], 0.0), jnp.where(incl, a_all[g][c:, n:], 0.0),
                   k_ht[g]), bd(v[g])) for g in groups]

    p = [jnp.where(base, a_ab[g], 0.0) for g in groups]
    inv = [jnp.where(eye, 1.0, 0.0) + p[g] for g in groups]
    p = [_mm(p[g], bd(p[g])) for g in groups]
    for _ in range(BASE_BLOCK.bit_length() - 3):
        both = [_mm(cat0(p[g], inv[g]), bd(p[g])) for g in groups]
        p = [both[g][:c] for g in groups]
        inv = [inv[g] + both[g][c:] for g in groups]
    inv = [inv[g] + _mm(inv[g], bd(p[g])) for g in groups]
    for off in offs:
        x = [_mm(jnp.where(off, a_ab[g], 0.0), bd(inv[g])) for g in groups]
        inv = [inv[g] + _mm(inv[g], bd(x[g])) for g in groups]

    u = [_mm(inv[g], bd(xs[g][:c] + av[g][:c])) for g in groups]
    yu = [_mm(cat0(jnp.where(incl, a_all[g][c:, :n], 0.0), b_ht[g]), bd(u[g])) for g in groups]
    y = [xs[g][c:2 * c] + yu[g][:c] + av[g][c:2 * c] for g in groups]
    s_new = [xs[g][2 * c:] + yu[g][c:] + av[g][2 * c:] for g in groups]
    return y, s_new


def _rwkv_scan_kernel(r_ref, k_ref, v_ref, lw_ref, kk_ref, a_ref, y_ref, s_ref):
    @pl.when(pl.program_id(1) == 0)
    def _():
        s_ref[...] = jnp.zeros_like(s_ref)

    nb, _, d = r_ref.shape
    where = [(bi, slice(lo, lo + GROUP_LANES)) for bi in range(nb) for lo in range(0, d, GROUP_LANES)]
    load = lambda ref: [ref[bi, :, sl] for bi, sl in where]
    y, s_new = _scan_groups(load(r_ref), load(k_ref), load(v_ref), load(lw_ref), load(kk_ref),
                            load(a_ref), [s_ref[g] for g in range(len(where))], _scan_masks())
    for g, (bi, sl) in enumerate(where):
        y_ref[bi, :, sl] = y[g]
        s_ref[g] = s_new[g]


def _rwkv_scan(r, k, v, lw, kk, a, *, batch_per_step=4):
    b, t, d = r.shape
    nb = batch_per_step
    tile = pl.BlockSpec((nb, CHUNK, d), lambda i, j: (i, j, 0))
    return pl.pallas_call(
        _rwkv_scan_kernel,
        out_shape=jax.ShapeDtypeStruct((b, t, d), F32),
        grid=(b // nb, t // CHUNK),
        in_specs=[tile] * 6,
        out_specs=tile,
        scratch_shapes=[pltpu.VMEM((nb * d // GROUP_LANES, HEAD, GROUP_LANES), F32)],
        compiler_params=pltpu.CompilerParams(
            dimension_semantics=("parallel", "arbitrary"), vmem_limit_bytes=VMEM_LIMIT),
        name="rwkv_scan",
    )(r, k, v, lw, kk, a)


def _attn_kernel(sinks_ref, h_ref, kp_ref, kc_ref, vtp_ref, vtc_ref, gpre_ref, wqt_ref, bqt_ref,
                 wo_ref, bo_ref, gpost_ref, o_ref):
    step = pl.program_id(1)
    w = WINDOW
    h = h_ref[0]
    u = _rms(h, gpre_ref[...])
    qt = ((_mm_nt(wqt_ref[...], u) + bqt_ref[...]) * (LOG2E * HEAD ** -0.5)).astype(BF16)
    k_cur, k_prev = kc_ref[0], kp_ref[0]
    vt_cur, vt_prev = vtc_ref[0], vtp_ref[0]
    kvw = k_cur.shape[1]
    lanes = GROUP * w

    si = lax.broadcasted_iota(jnp.int32, (2 * w, lanes), 0)
    qi = lax.broadcasted_iota(jnp.int32, (2 * w, lanes), 1) % w
    valid = (si > qi) & (si <= qi + w)
    valid_first = valid & ((si >= w) | (step > 0))
    seg = lax.broadcasted_iota(jnp.int32, (1, lanes), 1) // w

    def scores(n):
        k_band = (jnp.concatenate([k_prev, k_cur[:w]], axis=0) if n == 0
                  else k_cur[(n - 1) * w:(n + 1) * w])
        out = []
        for kh in range(KV_HEADS):
            q_stack = jnp.concatenate(
                [qt[(kh * GROUP + g) * HEAD:(kh * GROUP + g + 1) * HEAD, n * w:(n + 1) * w]
                 for g in range(GROUP)], axis=1)
            pieces = [q_stack]
            if kh > 0:
                pieces.insert(0, jnp.zeros((kh * HEAD, lanes), BF16))
            if kh < KV_HEADS - 1:
                pieces.append(jnp.zeros(((KV_HEADS - 1 - kh) * HEAD, lanes), BF16))
            out.append(jnp.dot(k_band, jnp.concatenate(pieces, axis=0), preferred_element_type=F32))
        return out

    def softmax(n, st):
        mask = valid_first if n == 0 else valid
        out = []
        for kh in range(KV_HEADS):
            sink = sinks_ref[kh * GROUP] * LOG2E
            for g in range(1, GROUP):
                sink = jnp.where(seg == g, sinks_ref[kh * GROUP + g] * LOG2E, sink)
            s = jnp.where(mask, st[kh], MASK_VALUE)
            mx = jnp.maximum(jnp.max(s, axis=0, keepdims=True), sink)
            p = jnp.exp2(s - mx)
            denom = jnp.sum(p, axis=0, keepdims=True) + jnp.exp2(sink - mx)
            out.append((p.astype(BF16), 1.0 / denom))
        return out

    def values(n, pd):
        vt_band = (jnp.concatenate([vt_prev, vt_cur[:, :w]], axis=1) if n == 0
                   else vt_cur[:, (n - 1) * w:(n + 1) * w])
        heads = []
        for kh in range(KV_HEADS):
            p, inv_denom = pd[kh]
            ot = jnp.dot(vt_band[kh * HEAD:(kh + 1) * HEAD], p, preferred_element_type=F32) * inv_denom
            heads += [ot[:, g * w:(g + 1) * w] for g in range(GROUP)]
        return jnp.concatenate(heads, axis=0).astype(BF16)

    st = [scores(0)]
    cols = []
    for n in range(ATTN_BLOCKS):
        if n + 1 < ATTN_BLOCKS:
            st.append(scores(n + 1))
        cols.append(values(n, softmax(n, st[n])))
    ot_all = jnp.concatenate(cols, axis=1)
    m = lax.dot_general(ot_all, wo_ref[...], (((0,), (0,)), ((), ())),
                        preferred_element_type=F32) + bo_ref[...]
    o_ref[0] = h + _rms(m, gpost_ref[...])


def _attn_block(h3, k3, vt3, sinks, g_pre, w_qt, b_qt, w_o, b_o, g_post):
    b, t, d = h3.shape
    kvw = k3.shape[-1]
    tq = ATTN_BLOCKS * WINDOW
    tile = pl.BlockSpec((1, tq, d), lambda i, j: (i, j, 0))
    prev = lambda j: jnp.maximum(j * ATTN_BLOCKS - 1, 0)
    params = (g_pre, w_qt, b_qt, w_o, b_o, g_post)
    return pl.pallas_call(
        _attn_kernel,
        out_shape=jax.ShapeDtypeStruct((b, t, d), F32),
        grid=(b, t // tq),
        in_specs=[
            pl.BlockSpec(memory_space=pltpu.SMEM),
            tile,
            pl.BlockSpec((1, WINDOW, kvw), lambda i, j: (i, prev(j), 0)),
            pl.BlockSpec((1, tq, kvw), lambda i, j: (i, j, 0)),
            pl.BlockSpec((1, kvw, WINDOW), lambda i, j: (i, 0, prev(j))),
            pl.BlockSpec((1, kvw, tq), lambda i, j: (i, 0, j)),
        ] + [_full_spec(p) for p in params],
        out_specs=tile,
        compiler_params=pltpu.CompilerParams(
            dimension_semantics=("parallel", "parallel"), vmem_limit_bytes=VMEM_LIMIT),
        name="swa_block",
    )(sinks, h3, k3, k3, vt3, vt3, *params)


def kernel(x, norm_g, ffn_w_in, ffn_w_out, rwkv_mu, rwkv_w_rkv, rwkv_w_o, rwkv_w0, rwkv_w1, rwkv_w2, rwkv_a0, rwkv_a1, rwkv_a2, rwkv_g1, rwkv_g2, rwkv_k_k, rwkv_k_a, rwkv_r_k, rwkv_gn_g, rwkv_gn_b, kv_norm_g, w_kv, b_kv, attn_w_q, attn_b_q, attn_w_o, attn_b_o, attn_sinks):
    b, t, d = x.shape
    m = b * t
    depth = norm_g.shape[0]
    n_a = rwkv_mu.shape[0]
    row = lambda vec: vec.reshape(1, -1).astype(F32)
    col = lambda vec: vec.reshape(-1, 1).astype(F32)
    bf = lambda w: w.astype(BF16)

    lane_head = jnp.arange(GROUP_LANES) // HEAD
    ones_bd = (lane_head[:, None] == lane_head[None, :]).astype(BF16)

    h = x.reshape(m, d)
    k_sh = vt_sh = None
    w_in, w_out = ffn_w_in.astype(F32), ffn_w_out.astype(F32)
    kvw = w_kv.shape[1] // 2
    kv_params = (row(kv_norm_g), bf(w_kv[:, :kvw]), row(b_kv[:kvw]), bf(w_kv[:, kvw:].T), col(b_kv[kvw:]))
    for layer in range(depth):
        g = norm_g[layer]
        h = _ffn_block(h, row(g[0]), w_in, w_out, row(g[1]), layer, 0)
        mix = None
        if layer < n_a:
            i = layer
            params = (row(g[2]), rwkv_mu[i], bf(rwkv_w_rkv[i]),
                      row(rwkv_w0[i]), bf(rwkv_w1[i]), bf(rwkv_w2[i]),
                      row(rwkv_a0[i]), bf(rwkv_a1[i]), bf(rwkv_a2[i]), bf(rwkv_g1[i]), bf(rwkv_g2[i]),
                      row(rwkv_k_k[i]), row(rwkv_k_a[i]), row(rwkv_r_k[i]), ones_bd)
            r, k, v, lw, kk, a, gate, bonus = _rwkv_prep(h.reshape(b, t, d), params)
            y = _rwkv_scan(r, k, v, lw, kk, a)
            mix = (y.reshape(m, d), bonus.reshape(m, d), gate.reshape(m, d), row(rwkv_gn_g[i]),
                   row(rwkv_gn_b[i]), bf(rwkv_w_o[i]), row(g[3]), ones_bd)
        else:
            j = layer - n_a
            h = _attn_block(h.reshape(b, t, d), k_sh, vt_sh, attn_sinks[j].astype(F32), row(g[2]),
                            bf(attn_w_q[j].T), col(attn_b_q[j]), bf(attn_w_o[j]), row(attn_b_o[j]),
                            row(g[3])).reshape(m, d)
        if layer == n_a - 1:
            h, k_sh, vt_sh = _ffn_block(h, row(g[4]), w_in, w_out, row(g[5]), layer, 1, mix=mix,
                                        kv=kv_params, seq_len=t)
            k_sh = k_sh.reshape(b, t, kvw)
        else:
            h = _ffn_block(h, row(g[4]), w_in, w_out, row(g[5]), layer, 1, mix=mix)
    return h.reshape(b, t, d)
```

```python
import functools

import jax
import jax.numpy as jnp
from jax import lax
from jax.experimental import pallas as pl
from jax.experimental.pallas import tpu as pltpu

F32 = jnp.float32
BF16 = jnp.bfloat16

RMS_EPS = 1e-6
GN_EPS = 64e-5
HEAD = 64
WINDOW = 128
MASK_VALUE = -1e30
KV_HEADS = 4
GROUP = 4
ATTN_BLOCKS = 4
LOG2E = 1.4426950408889634

CHUNK = 64
GROUP_LANES = 256
HEADS_PER_GROUP = GROUP_LANES // HEAD
BASE_BLOCK = 8
PREP_ROWS = 256
FFN_ROWS = 512
FFN_WEIGHT_STEPS = 11
EXP_MINUS_HALF = 0.6065306597126334

VMEM_LIMIT = 56 * 1024 * 1024
MXU_WIDTH = 256


def _rms(x, g):
    return x * lax.rsqrt(jnp.mean(x * x, axis=-1, keepdims=True) + RMS_EPS) * g


def _mm(a, b):
    return jnp.dot(a.astype(BF16), b.astype(BF16), preferred_element_type=F32)


def _mm_nt(a, b):
    return lax.dot_general(a.astype(BF16), b.astype(BF16), (((1,), (1,)), ((), ())),
                           preferred_element_type=F32)


def _split2(x):
    hi = x.astype(BF16)
    lo = (x - hi.astype(F32)).astype(BF16)
    return hi, lo


def _split3(x):
    h1 = x.astype(BF16)
    r1 = x - h1.astype(F32)
    h2 = r1.astype(BF16)
    h3 = (r1 - h2.astype(F32)).astype(BF16)
    return h1, h2, h3


def _seg_sum(x, ones_bd, pieces=1):
    out = []
    for lo in range(0, x.shape[1], GROUP_LANES):
        xs = x[:, lo:lo + GROUP_LANES]
        parts = (xs.astype(BF16),) if pieces == 1 else _split2(xs)
        out.append(sum(jnp.dot(p, ones_bd, preferred_element_type=F32) for p in parts))
    return jnp.concatenate(out, axis=1)


def _full_spec(x):
    return pl.BlockSpec(x.shape, lambda *_: (0,) * x.ndim)


def _ffn_chunks(d_ff):
    tiles = d_ff // MXU_WIDTH
    first = (tiles + 1) // 2 * MXU_WIDTH
    return ((0, first), (first, d_ff))


def _ffn_kernel(*refs, has_mix, has_kv):
    refs = list(refs)
    h_ref = refs.pop(0)
    mix_refs = [refs.pop(0) for _ in range(8)] if has_mix else None
    gpre_ref, win_chunk_ref, wo_chunk_ref, gpost_ref = [refs.pop(0) for _ in range(4)]
    kv_refs = [refs.pop(0) for _ in range(5)] if has_kv else None
    o_ref = refs.pop(0)
    win_ref, wo_ref = refs[-2:]
    d_ff = wo_ref.shape[0]
    step = pl.program_id(0)

    cw, cr = win_chunk_ref.shape[1], wo_chunk_ref.shape[0]
    for c in range(FFN_WEIGHT_STEPS):
        @pl.when(step == c)
        def _(c=c):
            win_ref[:, c * cw:(c + 1) * cw] = win_chunk_ref[...].astype(BF16)
            wo_ref[c * cr:(c + 1) * cr, :] = wo_chunk_ref[...].astype(BF16)

    @pl.when(step >= FFN_WEIGHT_STEPS)
    def _():
        _ffn_rows(h_ref, mix_refs, gpre_ref, win_ref, wo_ref, gpost_ref, kv_refs, o_ref,
                  refs[:-2], d_ff)


def _ffn_rows(h_ref, mix_refs, gpre_ref, win_ref, wo_ref, gpost_ref, kv_refs, o_ref, kv_out_refs, d_ff):
    has_mix, has_kv = mix_refs is not None, kv_refs is not None
    for r0 in range(0, h_ref.shape[0], FFN_ROWS):
        rows = slice(r0, r0 + FFN_ROWS)
        h = h_ref[rows, :]
        if has_mix:
            y_ref, bonus_ref, gate_ref, gng_ref, gnb_ref, wmix_ref, gmix_ref, ones_ref = mix_refs
            ones_bd = ones_ref[...]
            y = y_ref[rows, :]
            dev = y - _seg_sum(y, ones_bd, pieces=2) * (1.0 / HEAD)
            var = _seg_sum(dev * dev, ones_bd) * (1.0 / HEAD)
            yn = dev * lax.rsqrt(var + GN_EPS) * gng_ref[...] + gnb_ref[...] + bonus_ref[rows, :]
            h = h + _rms(_mm(yn * gate_ref[rows, :], wmix_ref[...]), gmix_ref[...])
        xn = _rms(h, gpre_ref[...]).astype(BF16)
        acc = None
        for lo, hi in _ffn_chunks(d_ff):
            gate = jnp.dot(xn, win_ref[:, lo:hi], preferred_element_type=F32)
            up = jnp.dot(xn, win_ref[:, d_ff + lo:d_ff + hi], preferred_element_type=F32)
            act = (gate * jax.nn.sigmoid(gate) * up).astype(BF16)
            part = jnp.dot(act, wo_ref[lo:hi, :], preferred_element_type=F32)
            acc = part if acc is None else acc + part
        h = h + 0.5 * _rms(acc, gpost_ref[...])
        o_ref[rows, :] = h
        if has_kv:
            gkv_ref, wk_ref, bk_ref, wvt_ref, bvt_ref = kv_refs
            k_ref, vt_ref = kv_out_refs
            u = _rms(h, gkv_ref[...]).astype(BF16)
            k_ref[rows, :] = (jnp.dot(u, wk_ref[...], preferred_element_type=F32) + bk_ref[...]).astype(BF16)
            vt_ref[0, :, rows] = (_mm_nt(wvt_ref[...], u) + bvt_ref[...]).astype(BF16)


def _ffn_block(h, g_pre, w_in, w_out, g_post, layer, which, *, mix=None, kv=None, seq_len=None):
    m, d = h.shape
    d_ff = w_out.shape[2]
    nw = FFN_WEIGHT_STEPS
    tm = FFN_ROWS if mix is not None else 2 * FFN_ROWS
    row_tile = lambda i: jnp.maximum(i - nw, 0)
    chunk = lambda i: jnp.minimum(i, nw - 1)
    tile = pl.BlockSpec((tm, d), lambda i: (row_tile(i), 0))
    resident = lambda x: pl.BlockSpec(x.shape, lambda i: (0,) * x.ndim, pipeline_mode=pl.Buffered(1))
    args, in_specs = [h], [tile]
    if mix is not None:
        args += list(mix)
        in_specs += [tile] * 3 + [resident(p) for p in mix[3:]]
    args += [g_pre, w_in, w_out, g_post]
    in_specs += [resident(g_pre),
                 pl.BlockSpec((None, None, d, 2 * d_ff // nw), lambda i: (layer, which, 0, chunk(i))),
                 pl.BlockSpec((None, None, d_ff // nw, d), lambda i: (layer, which, chunk(i), 0)),
                 resident(g_post)]
    out_shape, out_specs = [jax.ShapeDtypeStruct((m, d), F32)], [tile]
    if kv is not None:
        args += list(kv)
        in_specs += [resident(p) for p in kv]
        kvw = kv[1].shape[1]
        steps = seq_len // tm
        out_shape += [jax.ShapeDtypeStruct((m, kvw), BF16),
                      jax.ShapeDtypeStruct((m // seq_len, kvw, seq_len), BF16)]
        out_specs += [pl.BlockSpec((tm, kvw), lambda i: (row_tile(i), 0)),
                      pl.BlockSpec((1, kvw, tm), lambda i: (row_tile(i) // steps, 0, row_tile(i) % steps))]
    out = pl.pallas_call(
        functools.partial(_ffn_kernel, has_mix=mix is not None, has_kv=kv is not None),
        out_shape=out_shape,
        grid=(nw + m // tm,),
        in_specs=in_specs,
        out_specs=out_specs,
        scratch_shapes=[pltpu.VMEM((d, 2 * d_ff), BF16), pltpu.VMEM((d_ff, d), BF16)],
        compiler_params=pltpu.CompilerParams(
            dimension_semantics=("arbitrary",), vmem_limit_bytes=VMEM_LIMIT),
        name="ffn_block",
    )(*args)
    return out if kv is not None else out[0]


def _rwkv_prep_kernel(h_ref, g_ref, mu_ref, wrkv_ref, w0_ref, w1_ref, w2_ref,
                      a0_ref, a1_ref, a2_ref, g1_ref, g2_ref, kk_ref, ka_ref, rk_ref,
                      ones_ref,
                      r_out, k_out, v_out, lw_out, kk_out, a_out, g_out, bonus_out,
                      carry_ref):
    @pl.when(pl.program_id(1) == 0)
    def _():
        carry_ref[...] = jnp.zeros_like(carry_ref)

    ones_bd = ones_ref[...]
    mu = mu_ref[...]
    tm = h_ref.shape[1]
    row = lax.broadcasted_iota(jnp.int32, (PREP_ROWS, h_ref.shape[2]), 0)
    last = carry_ref[0:1, :]
    for r0 in range(0, tm, PREP_ROWS):
        rows = slice(r0, r0 + PREP_ROWS)
        u = _rms(h_ref[0, rows, :], g_ref[...])
        prev = jnp.where(row == 0, last, pltpu.roll(u, shift=1, axis=0))
        last = u[PREP_ROWS - 1:PREP_ROWS, :]
        xx = prev - u
        xr, xw, xk, xv, xa, xg = [u + xx * mu[i:i + 1, :] for i in range(6)]

        r = _mm(xr, wrkv_ref[0])
        k = _mm(xk, wrkv_ref[1])
        v = _mm(xv, wrkv_ref[2])
        x = w0_ref[...] + _mm(jnp.tanh(_mm(xw, w1_ref[...])), w2_ref[...])
        lw = jax.nn.sigmoid(x) * (-EXP_MINUS_HALF)
        a = jax.nn.sigmoid(a0_ref[...] + _mm(_mm(xa, a1_ref[...]), a2_ref[...]))
        g = _mm(jax.nn.sigmoid(_mm(xg, g1_ref[...])), g2_ref[...])

        kk = k * kk_ref[...]
        norm = jnp.sqrt(_seg_sum(kk * kk, ones_bd))
        kk = kk / jnp.maximum(norm, 1e-12)
        k = k * (1.0 + (a - 1.0) * ka_ref[...])

        r_out[0, rows, :] = r
        k_out[0, rows, :] = k
        v_out[0, rows, :] = v.astype(BF16)
        lw_out[0, rows, :] = lw
        kk_out[0, rows, :] = kk
        a_out[0, rows, :] = a
        g_out[0, rows, :] = g.astype(BF16)
        bonus_out[0, rows, :] = (_seg_sum(r * k * rk_ref[...], ones_bd) * v).astype(BF16)
    carry_ref[0:1, :] = last


def _rwkv_prep(h3, params, *, tm=512):
    b, t, d = h3.shape
    tile = pl.BlockSpec((1, tm, d), lambda i, j: (i, j, 0))
    resident = lambda x: pl.BlockSpec(x.shape, lambda i, j: (0,) * x.ndim, pipeline_mode=pl.Buffered(1))
    dtypes = [F32, F32, BF16, F32, F32, F32, BF16, BF16]
    return pl.pallas_call(
        _rwkv_prep_kernel,
        out_shape=[jax.ShapeDtypeStruct((b, t, d), dt) for dt in dtypes],
        grid=(b, t // tm),
        in_specs=[tile] + [resident(p) for p in params],
        out_specs=[tile] * 8,
        scratch_shapes=[pltpu.VMEM((8, d), F32)],
        compiler_params=pltpu.CompilerParams(
            dimension_semantics=("parallel", "arbitrary"), vmem_limit_bytes=VMEM_LIMIT),
        name="rwkv_prep",
    )(h3, *params)


def _scan_masks():
    c, n = CHUNK, GROUP_LANES
    row_s = lax.broadcasted_iota(jnp.int32, (c, n), 0)
    col_s = lax.broadcasted_iota(jnp.int32, (c, n), 1) % c
    strict = row_s > col_s
    incl = row_s >= col_s
    eye = row_s == col_s
    base = strict & ((row_s // BASE_BLOCK) == (col_s // BASE_BLOCK))
    offs = []
    b = BASE_BLOCK
    while b < c:
        offs.append(((row_s // (2 * b)) == (col_s // (2 * b)))
                    & ((row_s // b) % 2 == 1) & ((col_s // b) % 2 == 0))
        b *= 2
    row_b = lax.broadcasted_iota(jnp.int32, (n, n), 0)
    col_b = lax.broadcasted_iota(jnp.int32, (n, n), 1)
    mask_bd = (row_b // HEAD) == (col_b // HEAD)
    row_ge = {}
    shift = 1
    while shift < c:
        row_ge[shift] = row_s >= shift
        shift *= 2
    return mask_bd, strict, incl, eye, base, tuple(offs), row_ge


def _block_diag(z, mask_bd):
    tiled = jnp.concatenate([z] * HEADS_PER_GROUP, axis=0)
    return jnp.where(mask_bd, tiled, 0.0)


def _head_transpose(x):
    xt = x.T
    return jnp.concatenate([xt[h * HEAD:(h + 1) * HEAD, :] for h in range(HEADS_PER_GROUP)], axis=1)


def _scan_groups(r, k, v, lw, kk, a, s_cat, masks):
    mask_bd, strict, incl, eye, base, offs, row_ge = masks
    c, n = CHUNK, GROUP_LANES
    groups = range(len(r))
    bd = lambda z: _block_diag(z, mask_bd)
    cat0 = lambda *xs: jnp.concatenate(xs, axis=0)

    l_cum = list(lw)
    shift = 1
    while shift < c:
        l_cum = [l_cum[g] + jnp.where(row_ge[shift], pltpu.roll(l_cum[g], shift=shift, axis=0), 0.0)
                 for g in groups]
        shift *= 2
    e_l = [jnp.exp(l_cum[g]) for g in groups]
    e_nl = [jnp.exp(-l_cum[g]) for g in groups]
    a_t = [-kk[g] * jnp.exp(l_cum[g] - lw[g]) for g in groups]
    r_t = [r[g] * e_l[g] for g in groups]
    b_t = [kk[g] * a[g] * e_nl[g] for g in groups]
    k_t = [k[g] * e_nl[g] for g in groups]
    p_end = [e_l[g][c - 1:c, :] for g in groups]
    b_ht = [_head_transpose(b_t[g] * p_end[g]) for g in groups]
    k_ht = [_head_transpose(k_t[g] * p_end[g]) for g in groups]

    a_all = [_mm_nt(cat0(a_t[g], r_t[g]), cat0(bd(b_t[g]), bd(k_t[g]))) for g in groups]
    a_ab = [a_all[g][:c, :n] for g in groups]
    xs = [_mm(cat0(a_t[g], r_t[g], jnp.where(eye, p_end[g], 0.0)), bd(s_cat[g])) for g in groups]
    av = [_mm(cat0(jnp.where(strict, a_all[g][:c, n:], 0.0), jnp.where(incl, a_all[g][c:, n:], 0.0),
                   k_ht[g]), bd(v[g])) for g in groups]

    p = [jnp.where(base, a_ab[g], 0.0) for g in groups]
    inv = [jnp.where(eye, 1.0, 0.0) + p[g] for g in groups]
    p = [_mm(p[g], bd(p[g])) for g in groups]
    for _ in range(BASE_BLOCK.bit_length() - 3):
        both = [_mm(cat0(p[g], inv[g]), bd(p[g])) for g in groups]
        p = [both[g][:c] for g in groups]
        inv = [inv[g] + both[g][c:] for g in groups]
    inv = [inv[g] + _mm(inv[g], bd(p[g])) for g in groups]
    for off in offs:
        x = [_mm(jnp.where(off, a_ab[g], 0.0), bd(inv[g])) for g in groups]
        inv = [inv[g] + _mm(inv[g], bd(x[g])) for g in groups]

    u = [_mm(inv[g], bd(xs[g][:c] + av[g][:c])) for g in groups]
    yu = [_mm(cat0(jnp.where(incl, a_all[g][c:, :n], 0.0), b_ht[g]), bd(u[g])) for g in groups]
    y = [xs[g][c:2 * c] + yu[g][:c] + av[g][c:2 * c] for g in groups]
    s_new = [xs[g][2 * c:] + yu[g][c:] + av[g][2 * c:] for g in groups]
    return y, s_new


def _rwkv_scan_kernel(r_ref, k_ref, v_ref, lw_ref, kk_ref, a_ref, y_ref, s_ref):
    @pl.when(pl.program_id(1) == 0)
    def _():
        s_ref[...] = jnp.zeros_like(s_ref)

    nb, _, d = r_ref.shape
    where = [(bi, slice(lo, lo + GROUP_LANES)) for bi in range(nb) for lo in range(0, d, GROUP_LANES)]
    load = lambda ref: [ref[bi, :, sl] for bi, sl in where]
    y, s_new = _scan_groups(load(r_ref), load(k_ref), load(v_ref), load(lw_ref), load(kk_ref),
                            load(a_ref), [s_ref[g] for g in range(len(where))], _scan_masks())
    for g, (bi, sl) in enumerate(where):
        y_ref[bi, :, sl] = y[g]
        s_ref[g] = s_new[g]


def _rwkv_scan(r, k, v, lw, kk, a, *, batch_per_step=2):
    b, t, d = r.shape
    nb = batch_per_step
    tile = pl.BlockSpec((nb, CHUNK, d), lambda i, j: (i, j, 0))
    return pl.pallas_call(
        _rwkv_scan_kernel,
        out_shape=jax.ShapeDtypeStruct((b, t, d), F32),
        grid=(b // nb, t // CHUNK),
        in_specs=[tile] * 6,
        out_specs=tile,
        scratch_shapes=[pltpu.VMEM((nb * d // GROUP_LANES, HEAD, GROUP_LANES), F32)],
        compiler_params=pltpu.CompilerParams(
            dimension_semantics=("parallel", "arbitrary"), vmem_limit_bytes=VMEM_LIMIT),
        name="rwkv_scan",
    )(r, k, v, lw, kk, a)


def _attn_kernel(sinks_ref, h_ref, kp_ref, kc_ref, vtp_ref, vtc_ref, gpre_ref, wqt_ref, bqt_ref,
                 wo_ref, bo_ref, gpost_ref, o_ref):
    step = pl.program_id(1)
    w = WINDOW
    h = h_ref[0]
    u = _rms(h, gpre_ref[...]).astype(BF16)
    k_cur, k_prev = kc_ref[0], kp_ref[0]
    vt_cur, vt_prev = vtc_ref[0], vtp_ref[0]
    lanes = GROUP * w
    gw = GROUP * HEAD

    si = lax.broadcasted_iota(jnp.int32, (2 * w, lanes), 0)
    qi = lax.broadcasted_iota(jnp.int32, (2 * w, lanes), 1) % w
    valid = (si > qi) & (si <= qi + w)
    valid_first = valid & ((si >= w) | (step > 0))
    seg = lax.broadcasted_iota(jnp.int32, (1, lanes), 1) // w

    def q_proj(kh):
        rows = slice(kh * gw, (kh + 1) * gw)
        return ((_mm_nt(wqt_ref[rows, :], u) + bqt_ref[rows, :]) * (LOG2E * HEAD ** -0.5)).astype(BF16)

    def scores(qt_kh, kh, n):
        k_band = (jnp.concatenate([k_prev, k_cur[:w]], axis=0) if n == 0
                  else k_cur[(n - 1) * w:(n + 1) * w])
        q_stack = jnp.concatenate([qt_kh[g * HEAD:(g + 1) * HEAD, n * w:(n + 1) * w]
                                   for g in range(GROUP)], axis=1)
        pieces = [q_stack]
        if kh > 0:
            pieces.insert(0, jnp.zeros((kh * HEAD, lanes), BF16))
        if kh < KV_HEADS - 1:
            pieces.append(jnp.zeros(((KV_HEADS - 1 - kh) * HEAD, lanes), BF16))
        return jnp.dot(k_band, jnp.concatenate(pieces, axis=0), preferred_element_type=F32)

    def softmax(st, kh, n):
        sink = sinks_ref[kh * GROUP] * LOG2E
        for g in range(1, GROUP):
            sink = jnp.where(seg == g, sinks_ref[kh * GROUP + g] * LOG2E, sink)
        s = jnp.where(valid_first if n == 0 else valid, st, MASK_VALUE)
        mx = jnp.maximum(jnp.max(s, axis=0, keepdims=True), sink)
        p = jnp.exp2(s - mx)
        denom = jnp.sum(p, axis=0, keepdims=True) + jnp.exp2(sink - mx)
        return p.astype(BF16), 1.0 / denom

    def values(p, inv_denom, kh, n):
        vt_band = (jnp.concatenate([vt_prev, vt_cur[:, :w]], axis=1) if n == 0
                   else vt_cur[:, (n - 1) * w:(n + 1) * w])
        ot = jnp.dot(vt_band[kh * HEAD:(kh + 1) * HEAD], p, preferred_element_type=F32) * inv_denom
        return jnp.concatenate([ot[:, g * w:(g + 1) * w] for g in range(GROUP)], axis=0).astype(BF16)

    units = [(kh, n) for kh in range(KV_HEADS) for n in range(ATTN_BLOCKS)]
    ahead = 2
    qt = {0: q_proj(0)}
    st = {}
    for i in range(min(ahead, len(units))):
        st[i] = scores(qt[units[i][0]], *units[i])
    m = bo_ref[...]
    cols = []
    for i, (kh, n) in enumerate(units):
        if n == 0 and kh + 1 < KV_HEADS:
            qt[kh + 1] = q_proj(kh + 1)
        if i + ahead < len(units):
            st[i + ahead] = scores(qt[units[i + ahead][0]], *units[i + ahead])
        p, inv_denom = softmax(st.pop(i), kh, n)
        cols.append(values(p, inv_denom, kh, n))
        if n == ATTN_BLOCKS - 1:
            ot_kh = jnp.concatenate(cols, axis=1)
            cols = []
            m = m + lax.dot_general(ot_kh, wo_ref[kh * gw:(kh + 1) * gw, :], (((0,), (0,)), ((), ())),
                                    preferred_element_type=F32)
    o_ref[0] = h + _rms(m, gpost_ref[...])


def _attn_block(h3, k3, vt3, sinks, g_pre, w_qt, b_qt, w_o, b_o, g_post):
    b, t, d = h3.shape
    kvw = k3.shape[-1]
    tq = ATTN_BLOCKS * WINDOW
    tile = pl.BlockSpec((1, tq, d), lambda i, j: (i, j, 0))
    prev = lambda j: jnp.maximum(j * ATTN_BLOCKS - 1, 0)
    params = (g_pre, w_qt, b_qt, w_o, b_o, g_post)
    return pl.pallas_call(
        _attn_kernel,
        out_shape=jax.ShapeDtypeStruct((b, t, d), F32),
        grid=(b, t // tq),
        in_specs=[
            pl.BlockSpec(memory_space=pltpu.SMEM),
            tile,
            pl.BlockSpec((1, WINDOW, kvw), lambda i, j: (i, prev(j), 0)),
            pl.BlockSpec((1, tq, kvw), lambda i, j: (i, j, 0)),
            pl.BlockSpec((1, kvw, WINDOW), lambda i, j: (i, 0, prev(j))),
            pl.BlockSpec((1, kvw, tq), lambda i, j: (i, 0, j)),
        ] + [_full_spec(p) for p in params],
        out_specs=tile,
        compiler_params=pltpu.CompilerParams(
            dimension_semantics=("parallel", "parallel"), vmem_limit_bytes=VMEM_LIMIT),
        name="swa_block",
    )(sinks, h3, k3, k3, vt3, vt3, *params)


def kernel(x, norm_g, ffn_w_in, ffn_w_out, rwkv_mu, rwkv_w_rkv, rwkv_w_o, rwkv_w0, rwkv_w1, rwkv_w2, rwkv_a0, rwkv_a1, rwkv_a2, rwkv_g1, rwkv_g2, rwkv_k_k, rwkv_k_a, rwkv_r_k, rwkv_gn_g, rwkv_gn_b, kv_norm_g, w_kv, b_kv, attn_w_q, attn_b_q, attn_w_o, attn_b_o, attn_sinks):
    b, t, d = x.shape
    m = b * t
    depth = norm_g.shape[0]
    n_a = rwkv_mu.shape[0]
    row = lambda vec: vec.reshape(1, -1).astype(F32)
    col = lambda vec: vec.reshape(-1, 1).astype(F32)
    bf = lambda w: w.astype(BF16)

    lane_head = jnp.arange(GROUP_LANES) // HEAD
    ones_bd = (lane_head[:, None] == lane_head[None, :]).astype(BF16)

    h = x.reshape(m, d)
    k_sh = vt_sh = None
    w_in, w_out = ffn_w_in.astype(F32), ffn_w_out.astype(F32)
    kvw = w_kv.shape[1] // 2
    kv_params = (row(kv_norm_g), bf(w_kv[:, :kvw]), row(b_kv[:kvw]), bf(w_kv[:, kvw:].T), col(b_kv[kvw:]))
    for layer in range(depth):
        g = norm_g[layer]
        h = _ffn_block(h, row(g[0]), w_in, w_out, row(g[1]), layer, 0)
        mix = None
        if layer < n_a:
            i = layer
            params = (row(g[2]), rwkv_mu[i], bf(rwkv_w_rkv[i]),
                      row(rwkv_w0[i]), bf(rwkv_w1[i]), bf(rwkv_w2[i]),
                      row(rwkv_a0[i]), bf(rwkv_a1[i]), bf(rwkv_a2[i]), bf(rwkv_g1[i]), bf(rwkv_g2[i]),
                      row(rwkv_k_k[i]), row(rwkv_k_a[i]), row(rwkv_r_k[i]), ones_bd)
            r, k, v, lw, kk, a, gate, bonus = _rwkv_prep(h.reshape(b, t, d), params)
            y = _rwkv_scan(r, k, v, lw, kk, a)
            mix = (y.reshape(m, d), bonus.reshape(m, d), gate.reshape(m, d), row(rwkv_gn_g[i]),
                   row(rwkv_gn_b[i]), bf(rwkv_w_o[i]), row(g[3]), ones_bd)
        else:
            j = layer - n_a
            h = _attn_block(h.reshape(b, t, d), k_sh, vt_sh, attn_sinks[j].astype(F32), row(g[2]),
                            bf(attn_w_q[j].T), col(attn_b_q[j]), bf(attn_w_o[j]), row(attn_b_o[j]),
                            row(g[3])).reshape(m, d)
        if layer == n_a - 1:
            h, k_sh, vt_sh = _ffn_block(h, row(g[4]), w_in, w_out, row(g[5]), layer, 1, mix=mix,
                                        kv=kv_params, seq_len=t)
            k_sh = k_sh.reshape(b, t, kvw)
        else:
            h = _ffn_block(h, row(g[4]), w_in, w_out, row(g[5]), layer, 1, mix=mix)
    return h.reshape(b, t, d)
```

```python
import functools

import jax
import jax.numpy as jnp
from jax import lax
from jax.experimental import pallas as pl
from jax.experimental.pallas import tpu as pltpu

F32 = jnp.float32
BF16 = jnp.bfloat16

RMS_EPS = 1e-6
GN_EPS = 64e-5
HEAD = 64
WINDOW = 128
MASK_VALUE = -1e30
KV_HEADS = 4
GROUP = 4
ATTN_BLOCKS = 4
LOG2E = 1.4426950408889634

CHUNK = 64
GROUP_LANES = 256
HEADS_PER_GROUP = GROUP_LANES // HEAD
BASE_BLOCK = 8
PREP_ROWS = 512
FFN_ROWS = 512
FFN_WEIGHT_STEPS = 11
EXP_MINUS_HALF = 0.6065306597126334

VMEM_LIMIT = 56 * 1024 * 1024
MXU_WIDTH = 256


def _rms(x, g):
    return x * lax.rsqrt(jnp.mean(x * x, axis=-1, keepdims=True) + RMS_EPS) * g


def _mm(a, b):
    return jnp.dot(a.astype(BF16), b.astype(BF16), preferred_element_type=F32)


def _mm_nt(a, b):
    return lax.dot_general(a.astype(BF16), b.astype(BF16), (((1,), (1,)), ((), ())),
                           preferred_element_type=F32)


def _split2(x):
    hi = x.astype(BF16)
    lo = (x - hi.astype(F32)).astype(BF16)
    return hi, lo


def _split3(x):
    h1 = x.astype(BF16)
    r1 = x - h1.astype(F32)
    h2 = r1.astype(BF16)
    h3 = (r1 - h2.astype(F32)).astype(BF16)
    return h1, h2, h3


def _seg_sum(x, ones_bd, pieces=1):
    out = []
    for lo in range(0, x.shape[1], GROUP_LANES):
        xs = x[:, lo:lo + GROUP_LANES]
        parts = (xs.astype(BF16),) if pieces == 1 else _split2(xs)
        out.append(sum(jnp.dot(p, ones_bd, preferred_element_type=F32) for p in parts))
    return jnp.concatenate(out, axis=1)


def _full_spec(x):
    return pl.BlockSpec(x.shape, lambda *_: (0,) * x.ndim)


def _ffn_chunks(d_ff):
    tiles = d_ff // MXU_WIDTH
    first = (tiles + 1) // 2 * MXU_WIDTH
    return ((0, first), (first, d_ff))


def _ffn_kernel(*refs, has_mix, has_kv):
    refs = list(refs)
    h_ref = refs.pop(0)
    mix_refs = [refs.pop(0) for _ in range(8)] if has_mix else None
    gpre_ref, win_chunk_ref, wo_chunk_ref, gpost_ref = [refs.pop(0) for _ in range(4)]
    kv_refs = [refs.pop(0) for _ in range(5)] if has_kv else None
    o_ref = refs.pop(0)
    win_ref, wo_ref = refs[-2:]
    d_ff = wo_ref.shape[0]
    step = pl.program_id(0)

    cw, cr = win_chunk_ref.shape[1], wo_chunk_ref.shape[0]
    for c in range(FFN_WEIGHT_STEPS):
        @pl.when(step == c)
        def _(c=c):
            win_ref[:, c * cw:(c + 1) * cw] = win_chunk_ref[...].astype(BF16)
            wo_ref[c * cr:(c + 1) * cr, :] = wo_chunk_ref[...].astype(BF16)

    @pl.when(step >= FFN_WEIGHT_STEPS)
    def _():
        _ffn_rows(h_ref, mix_refs, gpre_ref, win_ref, wo_ref, gpost_ref, kv_refs, o_ref,
                  refs[:-2], d_ff)


def _ffn_rows(h_ref, mix_refs, gpre_ref, win_ref, wo_ref, gpost_ref, kv_refs, o_ref, kv_out_refs, d_ff):
    has_mix, has_kv = mix_refs is not None, kv_refs is not None
    for r0 in range(0, h_ref.shape[0], FFN_ROWS):
        rows = slice(r0, r0 + FFN_ROWS)
        h = h_ref[rows, :]
        if has_mix:
            y_ref, bonus_ref, gate_ref, gng_ref, gnb_ref, wmix_ref, gmix_ref, ones_ref = mix_refs
            ones_bd = ones_ref[...]
            y = y_ref[rows, :]
            dev = y - _seg_sum(y, ones_bd, pieces=2) * (1.0 / HEAD)
            var = _seg_sum(dev * dev, ones_bd) * (1.0 / HEAD)
            yn = dev * lax.rsqrt(var + GN_EPS) * gng_ref[...] + gnb_ref[...] + bonus_ref[rows, :]
            h = h + _rms(_mm(yn * gate_ref[rows, :], wmix_ref[...]), gmix_ref[...])
        xn = _rms(h, gpre_ref[...]).astype(BF16)
        acc = None
        for lo, hi in _ffn_chunks(d_ff):
            gate = jnp.dot(xn, win_ref[:, lo:hi], preferred_element_type=F32)
            up = jnp.dot(xn, win_ref[:, d_ff + lo:d_ff + hi], preferred_element_type=F32)
            act = (gate * jax.nn.sigmoid(gate) * up).astype(BF16)
            part = jnp.dot(act, wo_ref[lo:hi, :], preferred_element_type=F32)
            acc = part if acc is None else acc + part
        h = h + 0.5 * _rms(acc, gpost_ref[...])
        o_ref[rows, :] = h
        if has_kv:
            gkv_ref, wk_ref, bk_ref, wvt_ref, bvt_ref = kv_refs
            k_ref, vt_ref = kv_out_refs
            u = _rms(h, gkv_ref[...]).astype(BF16)
            k_ref[rows, :] = (jnp.dot(u, wk_ref[...], preferred_element_type=F32) + bk_ref[...]).astype(BF16)
            vt_ref[0, :, rows] = (_mm_nt(wvt_ref[...], u) + bvt_ref[...]).astype(BF16)


def _ffn_block(h, g_pre, w_in, w_out, g_post, layer, which, *, mix=None, kv=None, seq_len=None):
    m, d = h.shape
    d_ff = w_out.shape[2]
    nw = FFN_WEIGHT_STEPS
    tm = FFN_ROWS if mix is not None else 2 * FFN_ROWS
    row_tile = lambda i: jnp.maximum(i - nw, 0)
    chunk = lambda i: jnp.minimum(i, nw - 1)
    tile = pl.BlockSpec((tm, d), lambda i: (row_tile(i), 0))
    resident = lambda x: pl.BlockSpec(x.shape, lambda i: (0,) * x.ndim, pipeline_mode=pl.Buffered(1))
    args, in_specs = [h], [tile]
    if mix is not None:
        args += list(mix)
        in_specs += [tile] * 3 + [resident(p) for p in mix[3:]]
    args += [g_pre, w_in, w_out, g_post]
    in_specs += [resident(g_pre),
                 pl.BlockSpec((None, None, d, 2 * d_ff // nw), lambda i: (layer, which, 0, chunk(i))),
                 pl.BlockSpec((None, None, d_ff // nw, d), lambda i: (layer, which, chunk(i), 0)),
                 resident(g_post)]
    out_shape, out_specs = [jax.ShapeDtypeStruct((m, d), F32)], [tile]
    if kv is not None:
        args += list(kv)
        in_specs += [resident(p) for p in kv]
        kvw = kv[1].shape[1]
        steps = seq_len // tm
        out_shape += [jax.ShapeDtypeStruct((m, kvw), BF16),
                      jax.ShapeDtypeStruct((m // seq_len, kvw, seq_len), BF16)]
        out_specs += [pl.BlockSpec((tm, kvw), lambda i: (row_tile(i), 0)),
                      pl.BlockSpec((1, kvw, tm), lambda i: (row_tile(i) // steps, 0, row_tile(i) % steps))]
    out = pl.pallas_call(
        functools.partial(_ffn_kernel, has_mix=mix is not None, has_kv=kv is not None),
        out_shape=out_shape,
        grid=(nw + m // tm,),
        in_specs=in_specs,
        out_specs=out_specs,
        scratch_shapes=[pltpu.VMEM((d, 2 * d_ff), BF16), pltpu.VMEM((d_ff, d), BF16)],
        compiler_params=pltpu.CompilerParams(
            dimension_semantics=("arbitrary",), vmem_limit_bytes=VMEM_LIMIT),
        name="ffn_block",
    )(*args)
    return out if kv is not None else out[0]


def _rwkv_prep_kernel(h_ref, g_ref, mu_ref, wrkv_ref, w0_ref, w1_ref, w2_ref,
                      a0_ref, a1_ref, a2_ref, g1_ref, g2_ref, kk_ref, ka_ref, rk_ref,
                      ones_ref,
                      r_out, k_out, v_out, lw_out, kk_out, a_out, g_out, bonus_out,
                      carry_ref):
    @pl.when(pl.program_id(1) == 0)
    def _():
        carry_ref[...] = jnp.zeros_like(carry_ref)

    ones_bd = ones_ref[...]
    mu = mu_ref[...]
    tm = h_ref.shape[1]
    row = lax.broadcasted_iota(jnp.int32, (PREP_ROWS, h_ref.shape[2]), 0)
    last = carry_ref[0:1, :]
    for r0 in range(0, tm, PREP_ROWS):
        rows = slice(r0, r0 + PREP_ROWS)
        u = _rms(h_ref[0, rows, :], g_ref[...])
        prev = jnp.where(row == 0, last, pltpu.roll(u, shift=1, axis=0))
        last = u[PREP_ROWS - 1:PREP_ROWS, :]
        xx = prev - u
        xr, xw, xk, xv, xa, xg = [u + xx * mu[i:i + 1, :] for i in range(6)]

        w_mid = jnp.tanh(_mm(xw, w1_ref[...]))
        a_mid = _mm(xa, a1_ref[...])
        g_mid = jax.nn.sigmoid(_mm(xg, g1_ref[...]))
        r = _mm(xr, wrkv_ref[0])
        k = _mm(xk, wrkv_ref[1])
        v = _mm(xv, wrkv_ref[2])
        lw = jax.nn.sigmoid(w0_ref[...] + _mm(w_mid, w2_ref[...])) * (-EXP_MINUS_HALF)
        a = jax.nn.sigmoid(a0_ref[...] + _mm(a_mid, a2_ref[...]))
        g = _mm(g_mid, g2_ref[...])

        kk = k * kk_ref[...]
        norm = jnp.sqrt(_seg_sum(kk * kk, ones_bd))
        kk = kk / jnp.maximum(norm, 1e-12)
        k = k * (1.0 + (a - 1.0) * ka_ref[...])

        r_out[0, rows, :] = r
        k_out[0, rows, :] = k
        v_out[0, rows, :] = v.astype(BF16)
        lw_out[0, rows, :] = lw
        kk_out[0, rows, :] = kk
        a_out[0, rows, :] = a
        g_out[0, rows, :] = g.astype(BF16)
        bonus_out[0, rows, :] = (_seg_sum(r * k * rk_ref[...], ones_bd) * v).astype(BF16)
    carry_ref[0:1, :] = last


def _rwkv_prep(h3, params, *, tm=512):
    b, t, d = h3.shape
    tile = pl.BlockSpec((1, tm, d), lambda i, j: (i, j, 0))
    resident = lambda x: pl.BlockSpec(x.shape, lambda i, j: (0,) * x.ndim, pipeline_mode=pl.Buffered(1))
    dtypes = [F32, F32, BF16, F32, F32, F32, BF16, BF16]
    return pl.pallas_call(
        _rwkv_prep_kernel,
        out_shape=[jax.ShapeDtypeStruct((b, t, d), dt) for dt in dtypes],
        grid=(b, t // tm),
        in_specs=[tile] + [resident(p) for p in params],
        out_specs=[tile] * 8,
        scratch_shapes=[pltpu.VMEM((8, d), F32)],
        compiler_params=pltpu.CompilerParams(
            dimension_semantics=("parallel", "arbitrary"), vmem_limit_bytes=VMEM_LIMIT),
        name="rwkv_prep",
    )(h3, *params)


def _scan_masks():
    c, n = CHUNK, GROUP_LANES
    row_s = lax.broadcasted_iota(jnp.int32, (c, n), 0)
    col_s = lax.broadcasted_iota(jnp.int32, (c, n), 1) % c
    strict = row_s > col_s
    incl = row_s >= col_s
    eye = row_s == col_s
    base = strict & ((row_s // BASE_BLOCK) == (col_s // BASE_BLOCK))
    offs = []
    b = BASE_BLOCK
    while b < c:
        offs.append(((row_s // (2 * b)) == (col_s // (2 * b)))
                    & ((row_s // b) % 2 == 1) & ((col_s // b) % 2 == 0))
        b *= 2
    row_b = lax.broadcasted_iota(jnp.int32, (n, n), 0)
    col_b = lax.broadcasted_iota(jnp.int32, (n, n), 1)
    mask_bd = (row_b // HEAD) == (col_b // HEAD)
    row_ge = {}
    shift = 1
    while shift < c:
        row_ge[shift] = row_s >= shift
        shift *= 2
    return mask_bd, strict, incl, eye, base, tuple(offs), row_ge


def _block_diag(z, mask_bd):
    tiled = jnp.concatenate([z] * HEADS_PER_GROUP, axis=0)
    return jnp.where(mask_bd, tiled, 0.0)


def _head_transpose(x):
    xt = x.T
    return jnp.concatenate([xt[h * HEAD:(h + 1) * HEAD, :] for h in range(HEADS_PER_GROUP)], axis=1)


def _scan_groups(r, k, v, lw, kk, a, s_cat, masks):
    mask_bd, strict, incl, eye, base, offs, row_ge = masks
    c, n = CHUNK, GROUP_LANES
    groups = range(len(r))
    bd = lambda z: _block_diag(z, mask_bd)
    cat0 = lambda *xs: jnp.concatenate(xs, axis=0)

    l_cum = list(lw)
    shift = 1
    while shift < c:
        l_cum = [l_cum[g] + jnp.where(row_ge[shift], pltpu.roll(l_cum[g], shift=shift, axis=0), 0.0)
                 for g in groups]
        shift *= 2
    e_l = [jnp.exp(l_cum[g]) for g in groups]
    e_nl = [jnp.exp(-l_cum[g]) for g in groups]
    a_t = [-kk[g] * jnp.exp(l_cum[g] - lw[g]) for g in groups]
    r_t = [r[g] * e_l[g] for g in groups]
    b_t = [kk[g] * a[g] * e_nl[g] for g in groups]
    k_t = [k[g] * e_nl[g] for g in groups]
    p_end = [e_l[g][c - 1:c, :] for g in groups]
    b_ht = [_head_transpose(b_t[g] * p_end[g]) for g in groups]
    k_ht = [_head_transpose(k_t[g] * p_end[g]) for g in groups]

    a_all = [_mm_nt(cat0(a_t[g], r_t[g]), cat0(bd(b_t[g]), bd(k_t[g]))) for g in groups]
    a_ab = [a_all[g][:c, :n] for g in groups]
    xs = [_mm(cat0(a_t[g], r_t[g], jnp.where(eye, p_end[g], 0.0)), bd(s_cat[g])) for g in groups]
    av = [_mm(cat0(jnp.where(strict, a_all[g][:c, n:], 0.0), jnp.where(incl, a_all[g][c:, n:], 0.0),
                   k_ht[g]), bd(v[g])) for g in groups]

    p = [jnp.where(base, a_ab[g], 0.0) for g in groups]
    inv = [jnp.where(eye, 1.0, 0.0) + p[g] for g in groups]
    p = [_mm(p[g], bd(p[g])) for g in groups]
    for _ in range(BASE_BLOCK.bit_length() - 3):
        both = [_mm(cat0(p[g], inv[g]), bd(p[g])) for g in groups]
        p = [both[g][:c] for g in groups]
        inv = [inv[g] + both[g][c:] for g in groups]
    inv = [inv[g] + _mm(inv[g], bd(p[g])) for g in groups]
    for off in offs:
        x = [_mm(jnp.where(off, a_ab[g], 0.0), bd(inv[g])) for g in groups]
        inv = [inv[g] + _mm(inv[g], bd(x[g])) for g in groups]

    u = [_mm(inv[g], bd(xs[g][:c] + av[g][:c])) for g in groups]
    yu = [_mm(cat0(jnp.where(incl, a_all[g][c:, :n], 0.0), b_ht[g]), bd(u[g])) for g in groups]
    y = [xs[g][c:2 * c] + yu[g][:c] + av[g][c:2 * c] for g in groups]
    s_new = [xs[g][2 * c:] + yu[g][c:] + av[g][2 * c:] for g in groups]
    return y, s_new


def _rwkv_scan_kernel(r_ref, k_ref, v_ref, lw_ref, kk_ref, a_ref, y_ref, s_ref):
    @pl.when(pl.program_id(1) == 0)
    def _():
        s_ref[...] = jnp.zeros_like(s_ref)

    nb, _, d = r_ref.shape
    where = [(bi, slice(lo, lo + GROUP_LANES)) for bi in range(nb) for lo in range(0, d, GROUP_LANES)]
    load = lambda ref: [ref[bi, :, sl] for bi, sl in where]
    y, s_new = _scan_groups(load(r_ref), load(k_ref), load(v_ref), load(lw_ref), load(kk_ref),
                            load(a_ref), [s_ref[g] for g in range(len(where))], _scan_masks())
    for g, (bi, sl) in enumerate(where):
        y_ref[bi, :, sl] = y[g]
        s_ref[g] = s_new[g]


def _rwkv_scan(r, k, v, lw, kk, a, *, batch_per_step=4):
    b, t, d = r.shape
    nb = batch_per_step
    tile = pl.BlockSpec((nb, CHUNK, d), lambda i, j: (i, j, 0))
    return pl.pallas_call(
        _rwkv_scan_kernel,
        out_shape=jax.ShapeDtypeStruct((b, t, d), F32),
        grid=(b // nb, t // CHUNK),
        in_specs=[tile] * 6,
        out_specs=tile,
        scratch_shapes=[pltpu.VMEM((nb * d // GROUP_LANES, HEAD, GROUP_LANES), F32)],
        compiler_params=pltpu.CompilerParams(
            dimension_semantics=("parallel", "arbitrary"), vmem_limit_bytes=VMEM_LIMIT),
        name="rwkv_scan",
    )(r, k, v, lw, kk, a)


def _attn_kernel(sinks_ref, h_ref, kp_ref, kc_ref, vtp_ref, vtc_ref, gpre_ref, wqt_ref, bqt_ref,
                 wo_ref, bo_ref, gpost_ref, o_ref):
    step = pl.program_id(1)
    w = WINDOW
    h = h_ref[0]
    u = _rms(h, gpre_ref[...])
    qt = ((_mm_nt(wqt_ref[...], u) + bqt_ref[...]) * (LOG2E * HEAD ** -0.5)).astype(BF16)
    k_cur, k_prev = kc_ref[0], kp_ref[0]
    vt_cur, vt_prev = vtc_ref[0], vtp_ref[0]
    lanes = GROUP * w

    si = lax.broadcasted_iota(jnp.int32, (2 * w, lanes), 0)
    qi = lax.broadcasted_iota(jnp.int32, (2 * w, lanes), 1) % w
    valid = (si > qi) & (si <= qi + w)
    valid_first = valid & ((si >= w) | (step > 0))
    seg = lax.broadcasted_iota(jnp.int32, (1, lanes), 1) // w

    def scores(n):
        k_band = (jnp.concatenate([k_prev, k_cur[:w]], axis=0) if n == 0
                  else k_cur[(n - 1) * w:(n + 1) * w])
        out = []
        for kh in range(KV_HEADS):
            q_stack = jnp.concatenate(
                [qt[(kh * GROUP + g) * HEAD:(kh * GROUP + g + 1) * HEAD, n * w:(n + 1) * w]
                 for g in range(GROUP)], axis=1)
            pieces = [q_stack]
            if kh > 0:
                pieces.insert(0, jnp.zeros((kh * HEAD, lanes), BF16))
            if kh < KV_HEADS - 1:
                pieces.append(jnp.zeros(((KV_HEADS - 1 - kh) * HEAD, lanes), BF16))
            out.append(jnp.dot(k_band, jnp.concatenate(pieces, axis=0), preferred_element_type=F32))
        return out

    def softmax(n, st):
        mask = valid_first if n == 0 else valid
        out = []
        for kh in range(KV_HEADS):
            sink = sinks_ref[kh * GROUP] * LOG2E
            for g in range(1, GROUP):
                sink = jnp.where(seg == g, sinks_ref[kh * GROUP + g] * LOG2E, sink)
            s = jnp.where(mask, st[kh], MASK_VALUE)
            mx = jnp.maximum(jnp.max(s, axis=0, keepdims=True), sink)
            p = jnp.exp2(s - mx)
            denom = jnp.sum(p, axis=0, keepdims=True) + jnp.exp2(sink - mx)
            out.append((p.astype(BF16), 1.0 / denom))
        return out

    def values(n, pd):
        vt_band = (jnp.concatenate([vt_prev, vt_cur[:, :w]], axis=1) if n == 0
                   else vt_cur[:, (n - 1) * w:(n + 1) * w])
        heads = []
        for kh in range(KV_HEADS):
            p, inv_denom = pd[kh]
            ot = jnp.dot(vt_band[kh * HEAD:(kh + 1) * HEAD], p, preferred_element_type=F32) * inv_denom
            heads += [ot[:, g * w:(g + 1) * w] for g in range(GROUP)]
        return jnp.concatenate(heads, axis=0).astype(BF16)

    st = [scores(0)]
    cols = []
    for n in range(ATTN_BLOCKS):
        if n + 1 < ATTN_BLOCKS:
            st.append(scores(n + 1))
        cols.append(values(n, softmax(n, st[n])))
    ot_all = jnp.concatenate(cols, axis=1)
    m = lax.dot_general(ot_all, wo_ref[...], (((0,), (0,)), ((), ())),
                        preferred_element_type=F32) + bo_ref[...]
    o_ref[0] = h + _rms(m, gpost_ref[...])


def _attn_block(h3, k3, vt3, sinks, g_pre, w_qt, b_qt, w_o, b_o, g_post):
    b, t, d = h3.shape
    kvw = k3.shape[-1]
    tq = ATTN_BLOCKS * WINDOW
    tile = pl.BlockSpec((1, tq, d), lambda i, j: (i, j, 0))
    prev = lambda j: jnp.maximum(j * ATTN_BLOCKS - 1, 0)
    params = (g_pre, w_qt, b_qt, w_o, b_o, g_post)
    return pl.pallas_call(
        _attn_kernel,
        out_shape=jax.ShapeDtypeStruct((b, t, d), F32),
        grid=(b, t // tq),
        in_specs=[
            pl.BlockSpec(memory_space=pltpu.SMEM),
            tile,
            pl.BlockSpec((1, WINDOW, kvw), lambda i, j: (i, prev(j), 0)),
            pl.BlockSpec((1, tq, kvw), lambda i, j: (i, j, 0)),
            pl.BlockSpec((1, kvw, WINDOW), lambda i, j: (i, 0, prev(j))),
            pl.BlockSpec((1, kvw, tq), lambda i, j: (i, 0, j)),
        ] + [_full_spec(p) for p in params],
        out_specs=tile,
        compiler_params=pltpu.CompilerParams(
            dimension_semantics=("parallel", "parallel"), vmem_limit_bytes=VMEM_LIMIT),
        name="swa_block",
    )(sinks, h3, k3, k3, vt3, vt3, *params)


def kernel(x, norm_g, ffn_w_in, ffn_w_out, rwkv_mu, rwkv_w_rkv, rwkv_w_o, rwkv_w0, rwkv_w1, rwkv_w2, rwkv_a0, rwkv_a1, rwkv_a2, rwkv_g1, rwkv_g2, rwkv_k_k, rwkv_k_a, rwkv_r_k, rwkv_gn_g, rwkv_gn_b, kv_norm_g, w_kv, b_kv, attn_w_q, attn_b_q, attn_w_o, attn_b_o, attn_sinks):
    b, t, d = x.shape
    m = b * t
    depth = norm_g.shape[0]
    n_a = rwkv_mu.shape[0]
    row = lambda vec: vec.reshape(1, -1).astype(F32)
    col = lambda vec: vec.reshape(-1, 1).astype(F32)
    bf = lambda w: w.astype(BF16)

    lane_head = jnp.arange(GROUP_LANES) // HEAD
    ones_bd = (lane_head[:, None] == lane_head[None, :]).astype(BF16)

    h = x.reshape(m, d)
    k_sh = vt_sh = None
    w_in, w_out = ffn_w_in.astype(F32), ffn_w_out.astype(F32)
    kvw = w_kv.shape[1] // 2
    kv_params = (row(kv_norm_g), bf(w_kv[:, :kvw]), row(b_kv[:kvw]), bf(w_kv[:, kvw:].T), col(b_kv[kvw:]))
    for layer in range(depth):
        g = norm_g[layer]
        h = _ffn_block(h, row(g[0]), w_in, w_out, row(g[1]), layer, 0)
        mix = None
        if layer < n_a:
            i = layer
            params = (row(g[2]), rwkv_mu[i], bf(rwkv_w_rkv[i]),
                      row(rwkv_w0[i]), bf(rwkv_w1[i]), bf(rwkv_w2[i]),
                      row(rwkv_a0[i]), bf(rwkv_a1[i]), bf(rwkv_a2[i]), bf(rwkv_g1[i]), bf(rwkv_g2[i]),
                      row(rwkv_k_k[i]), row(rwkv_k_a[i]), row(rwkv_r_k[i]), ones_bd)
            r, k, v, lw, kk, a, gate, bonus = _rwkv_prep(h.reshape(b, t, d), params)
            y = _rwkv_scan(r, k, v, lw, kk, a)
            mix = (y.reshape(m, d), bonus.reshape(m, d), gate.reshape(m, d), row(rwkv_gn_g[i]),
                   row(rwkv_gn_b[i]), bf(rwkv_w_o[i]), row(g[3]), ones_bd)
        else:
            j = layer - n_a
            h = _attn_block(h.reshape(b, t, d), k_sh, vt_sh, attn_sinks[j].astype(F32), row(g[2]),
                            bf(attn_w_q[j].T), col(attn_b_q[j]), bf(attn_w_o[j]), row(attn_b_o[j]),
                            row(g[3])).reshape(m, d)
        if layer == n_a - 1:
            h, k_sh, vt_sh = _ffn_block(h, row(g[4]), w_in, w_out, row(g[5]), layer, 1, mix=mix,
                                        kv=kv_params, seq_len=t)
            k_sh = k_sh.reshape(b, t, kvw)
        else:
            h = _ffn_block(h, row(g[4]), w_in, w_out, row(g[5]), layer, 1, mix=mix)
    return h.reshape(b, t, d)
```

```python
import functools

import jax
import jax.numpy as jnp
from jax import lax
from jax.experimental import pallas as pl
from jax.experimental.pallas import tpu as pltpu

F32 = jnp.float32
BF16 = jnp.bfloat16

RMS_EPS = 1e-6
GN_EPS = 64e-5
HEAD = 64
WINDOW = 128
MASK_VALUE = -1e30
KV_HEADS = 4
GROUP = 4
ATTN_BLOCKS = 4
LOG2E = 1.4426950408889634

CHUNK = 64
GROUP_LANES = 256
HEADS_PER_GROUP = GROUP_LANES // HEAD
BASE_BLOCK = 8
PREP_ROWS = 512
FFN_ROWS = 1024
FFN_TILE = 1024
FFN_MIX_TILE = 512
FFN_WEIGHT_STEPS = 11
EXP_MINUS_HALF = 0.6065306597126334

VMEM_LIMIT = 56 * 1024 * 1024
MXU_WIDTH = 256


def _rms(x, g):
    return x * lax.rsqrt(jnp.mean(x * x, axis=-1, keepdims=True) + RMS_EPS) * g


def _mm(a, b):
    return jnp.dot(a.astype(BF16), b.astype(BF16), preferred_element_type=F32)


def _mm_nt(a, b):
    return lax.dot_general(a.astype(BF16), b.astype(BF16), (((1,), (1,)), ((), ())),
                           preferred_element_type=F32)


def _split2(x):
    hi = x.astype(BF16)
    lo = (x - hi.astype(F32)).astype(BF16)
    return hi, lo


def _split3(x):
    h1 = x.astype(BF16)
    r1 = x - h1.astype(F32)
    h2 = r1.astype(BF16)
    h3 = (r1 - h2.astype(F32)).astype(BF16)
    return h1, h2, h3


def _seg_sum(x, ones_bd, pieces=1):
    m, d = x.shape
    xs = jnp.concatenate([x[:, lo:lo + GROUP_LANES] for lo in range(0, d, GROUP_LANES)], axis=0)
    parts = (xs.astype(BF16),) if pieces == 1 else _split2(xs)
    s = sum(jnp.dot(p, ones_bd, preferred_element_type=F32) for p in parts)
    return jnp.concatenate([s[i * m:(i + 1) * m] for i in range(d // GROUP_LANES)], axis=1)


def _full_spec(x):
    return pl.BlockSpec(x.shape, lambda *_: (0,) * x.ndim)


def _ffn_chunks(d_ff):
    tiles = d_ff // MXU_WIDTH
    first = (tiles + 1) // 2 * MXU_WIDTH
    return ((0, first), (first, d_ff))


def _ffn_kernel(*refs, has_mix, has_kv):
    refs = list(refs)
    h_ref = refs.pop(0)
    mix_refs = [refs.pop(0) for _ in range(8)] if has_mix else None
    gpre_ref, win_chunk_ref, wo_chunk_ref, gpost_ref = [refs.pop(0) for _ in range(4)]
    kv_refs = [refs.pop(0) for _ in range(3)] if has_kv else None
    o_ref = refs.pop(0)
    win_ref, wo_ref = refs[-2:]
    d_ff = wo_ref.shape[0]
    step = pl.program_id(0)

    cw, cr = win_chunk_ref.shape[1], wo_chunk_ref.shape[0]
    for c in range(FFN_WEIGHT_STEPS):
        @pl.when(step == c)
        def _(c=c):
            win_ref[:, c * cw:(c + 1) * cw] = win_chunk_ref[...].astype(BF16)
            wo_ref[c * cr:(c + 1) * cr, :] = wo_chunk_ref[...].astype(BF16)

    @pl.when(step >= FFN_WEIGHT_STEPS)
    def _():
        _ffn_rows(h_ref, mix_refs, gpre_ref, win_ref, wo_ref, gpost_ref, kv_refs, o_ref,
                  refs[:-2], d_ff)


def _ffn_rows(h_ref, mix_refs, gpre_ref, win_ref, wo_ref, gpost_ref, kv_refs, o_ref, kv_out_refs, d_ff):
    has_mix, has_kv = mix_refs is not None, kv_refs is not None
    sub = min(FFN_ROWS, h_ref.shape[0])
    for r0 in range(0, h_ref.shape[0], sub):
        rows = slice(r0, r0 + sub)
        h = h_ref[rows, :]
        if has_mix:
            y_ref, bonus_ref, gate_ref, gng_ref, gnb_ref, wmix_ref, gmix_ref, ones_ref = mix_refs
            ones_bd = ones_ref[...]
            y = y_ref[rows, :]
            dev = y - _seg_sum(y, ones_bd, pieces=2) * (1.0 / HEAD)
            var = _seg_sum(dev * dev, ones_bd) * (1.0 / HEAD)
            yn = dev * lax.rsqrt(var + GN_EPS) * gng_ref[...] + gnb_ref[...] + bonus_ref[rows, :]
            h = h + _rms(_mm(yn * gate_ref[rows, :], wmix_ref[...]), gmix_ref[...])
        xn = _rms(h, gpre_ref[...]).astype(BF16)
        acc = None
        for lo, hi in _ffn_chunks(d_ff):
            gate = jnp.dot(xn, win_ref[:, lo:hi], preferred_element_type=F32)
            up = jnp.dot(xn, win_ref[:, d_ff + lo:d_ff + hi], preferred_element_type=F32)
            act = (gate * jax.nn.sigmoid(gate) * up).astype(BF16)
            part = jnp.dot(act, wo_ref[lo:hi, :], preferred_element_type=F32)
            acc = part if acc is None else acc + part
        h = h + 0.5 * _rms(acc, gpost_ref[...])
        o_ref[rows, :] = h
        if has_kv:
            gkv_ref, wkvt_ref, bkvt_ref = kv_refs
            (kvt_ref,) = kv_out_refs
            u = _rms(h, gkv_ref[...]).astype(BF16)
            kvt_ref[0, :, rows] = (_mm_nt(wkvt_ref[...], u) + bkvt_ref[...]).astype(BF16)


def _ffn_block(h, g_pre, w_in, w_out, g_post, layer, which, *, mix=None, kv=None, seq_len=None):
    m, d = h.shape
    d_ff = w_out.shape[2]
    nw = FFN_WEIGHT_STEPS
    tm = FFN_MIX_TILE if mix is not None else FFN_TILE
    row_tile = lambda i: jnp.maximum(i - nw, 0)
    chunk = lambda i: jnp.minimum(i, nw - 1)
    tile = pl.BlockSpec((tm, d), lambda i: (row_tile(i), 0))
    resident = lambda x: pl.BlockSpec(x.shape, lambda i: (0,) * x.ndim, pipeline_mode=pl.Buffered(1))
    args, in_specs = [h], [tile]
    if mix is not None:
        args += list(mix)
        in_specs += [tile] * 3 + [resident(p) for p in mix[3:]]
    args += [g_pre, w_in, w_out, g_post]
    in_specs += [resident(g_pre),
                 pl.BlockSpec((None, None, d, 2 * d_ff // nw), lambda i: (layer, which, 0, chunk(i))),
                 pl.BlockSpec((None, None, d_ff // nw, d), lambda i: (layer, which, chunk(i), 0)),
                 resident(g_post)]
    out_shape, out_specs = [jax.ShapeDtypeStruct((m, d), F32)], [tile]
    if kv is not None:
        args += list(kv)
        in_specs += [resident(p) for p in kv]
        kv2 = kv[1].shape[0]
        steps = seq_len // tm
        out_shape += [jax.ShapeDtypeStruct((m // seq_len, kv2, seq_len), BF16)]
        out_specs += [pl.BlockSpec((1, kv2, tm), lambda i: (row_tile(i) // steps, 0, row_tile(i) % steps))]
    out = pl.pallas_call(
        functools.partial(_ffn_kernel, has_mix=mix is not None, has_kv=kv is not None),
        out_shape=out_shape,
        grid=(nw + m // tm,),
        in_specs=in_specs,
        out_specs=out_specs,
        scratch_shapes=[pltpu.VMEM((d, 2 * d_ff), BF16), pltpu.VMEM((d_ff, d), BF16)],
        compiler_params=pltpu.CompilerParams(
            dimension_semantics=("arbitrary",), vmem_limit_bytes=VMEM_LIMIT),
        name="ffn_block",
    )(*args)
    return out if kv is not None else out[0]


def _rwkv_prep_kernel(h_ref, g_ref, mu_ref, wrkv_ref, w0_ref, w1_ref, w2_ref,
                      a0_ref, a1_ref, a2_ref, g1_ref, g2_ref, kk_ref, ka_ref, rk_ref,
                      ones_ref,
                      r_out, k_out, v_out, lw_out, kk_out, a_out, g_out, bonus_out,
                      carry_ref):
    @pl.when(pl.program_id(1) == 0)
    def _():
        carry_ref[...] = jnp.zeros_like(carry_ref)

    ones_bd = ones_ref[...]
    mu = mu_ref[...]
    tm = h_ref.shape[1]
    row = lax.broadcasted_iota(jnp.int32, (PREP_ROWS, h_ref.shape[2]), 0)
    last = carry_ref[0:1, :]
    for r0 in range(0, tm, PREP_ROWS):
        rows = slice(r0, r0 + PREP_ROWS)
        u = _rms(h_ref[0, rows, :], g_ref[...])
        prev = jnp.where(row == 0, last, pltpu.roll(u, shift=1, axis=0))
        last = u[PREP_ROWS - 1:PREP_ROWS, :]
        xx = prev - u
        xr, xw, xk, xv, xa, xg = [u + xx * mu[i:i + 1, :] for i in range(6)]

        w_mid = jnp.tanh(_mm(xw, w1_ref[...]))
        a_mid = _mm(xa, a1_ref[...])
        g_mid = jax.nn.sigmoid(_mm(xg, g1_ref[...]))
        r = _mm(xr, wrkv_ref[0])
        k = _mm(xk, wrkv_ref[1])
        v = _mm(xv, wrkv_ref[2])
        lw = jax.nn.sigmoid(w0_ref[...] + _mm(w_mid, w2_ref[...])) * (-EXP_MINUS_HALF)
        a = jax.nn.sigmoid(a0_ref[...] + _mm(a_mid, a2_ref[...]))
        g = _mm(g_mid, g2_ref[...])

        kk = k * kk_ref[...]
        norm = jnp.sqrt(_seg_sum(kk * kk, ones_bd))
        kk = kk / jnp.maximum(norm, 1e-12)
        k = k * (1.0 + (a - 1.0) * ka_ref[...])

        r_out[0, rows, :] = r
        k_out[0, rows, :] = k
        v_out[0, rows, :] = v.astype(BF16)
        lw_out[0, rows, :] = lw
        kk_out[0, rows, :] = kk
        a_out[0, rows, :] = a
        g_out[0, rows, :] = g.astype(BF16)
        bonus_out[0, rows, :] = (_seg_sum(r * k * rk_ref[...], ones_bd) * v).astype(BF16)
    carry_ref[0:1, :] = last


def _rwkv_prep(h3, params, *, tm=512):
    b, t, d = h3.shape
    tile = pl.BlockSpec((1, tm, d), lambda i, j: (i, j, 0))
    resident = lambda x: pl.BlockSpec(x.shape, lambda i, j: (0,) * x.ndim, pipeline_mode=pl.Buffered(1))
    dtypes = [F32, F32, BF16, F32, F32, F32, BF16, BF16]
    return pl.pallas_call(
        _rwkv_prep_kernel,
        out_shape=[jax.ShapeDtypeStruct((b, t, d), dt) for dt in dtypes],
        grid=(b, t // tm),
        in_specs=[tile] + [resident(p) for p in params],
        out_specs=[tile] * 8,
        scratch_shapes=[pltpu.VMEM((8, d), F32)],
        compiler_params=pltpu.CompilerParams(
            dimension_semantics=("parallel", "arbitrary"), vmem_limit_bytes=VMEM_LIMIT),
        name="rwkv_prep",
    )(h3, *params)


def _scan_masks():
    c, n = CHUNK, GROUP_LANES
    row_s = lax.broadcasted_iota(jnp.int32, (c, n), 0)
    col_s = lax.broadcasted_iota(jnp.int32, (c, n), 1) % c
    strict = row_s > col_s
    incl = row_s >= col_s
    eye = row_s == col_s
    base = strict & ((row_s // BASE_BLOCK) == (col_s // BASE_BLOCK))
    offs = []
    b = BASE_BLOCK
    while b < c:
        offs.append(((row_s // (2 * b)) == (col_s // (2 * b)))
                    & ((row_s // b) % 2 == 1) & ((col_s // b) % 2 == 0))
        b *= 2
    row_b = lax.broadcasted_iota(jnp.int32, (n, n), 0)
    col_b = lax.broadcasted_iota(jnp.int32, (n, n), 1)
    mask_bd = (row_b // HEAD) == (col_b // HEAD)
    row_ge = {}
    shift = 1
    while shift < c:
        row_ge[shift] = row_s >= shift
        shift *= 2
    return mask_bd, strict, incl, eye, base, tuple(offs), row_ge


def _block_diag(z, mask_bd):
    tiled = jnp.concatenate([z] * HEADS_PER_GROUP, axis=0)
    return jnp.where(mask_bd, tiled, 0.0)


def _head_transpose(x):
    xt = x.T
    return jnp.concatenate([xt[h * HEAD:(h + 1) * HEAD, :] for h in range(HEADS_PER_GROUP)], axis=1)


def _scan_groups(r, k, v, lw, kk, a, s_cat, masks):
    mask_bd, strict, incl, eye, base, offs, row_ge = masks
    c, n = CHUNK, GROUP_LANES
    groups = range(len(r))
    bd = lambda z: _block_diag(z, mask_bd)
    cat0 = lambda *xs: jnp.concatenate(xs, axis=0)

    l_cum = list(lw)
    shift = 1
    while shift < c:
        l_cum = [l_cum[g] + jnp.where(row_ge[shift], pltpu.roll(l_cum[g], shift=shift, axis=0), 0.0)
                 for g in groups]
        shift *= 2
    e_l = [jnp.exp(l_cum[g]) for g in groups]
    e_nl = [jnp.exp(-l_cum[g]) for g in groups]
    a_t = [-kk[g] * jnp.exp(l_cum[g] - lw[g]) for g in groups]
    r_t = [r[g] * e_l[g] for g in groups]
    b_t = [kk[g] * a[g] * e_nl[g] for g in groups]
    k_t = [k[g] * e_nl[g] for g in groups]
    p_end = [e_l[g][c - 1:c, :] for g in groups]
    b_ht = [_head_transpose(b_t[g] * p_end[g]) for g in groups]
    k_ht = [_head_transpose(k_t[g] * p_end[g]) for g in groups]

    a_all = [_mm_nt(cat0(a_t[g], r_t[g]), cat0(bd(b_t[g]), bd(k_t[g]))) for g in groups]
    a_ab = [a_all[g][:c, :n] for g in groups]
    xs = [_mm(cat0(a_t[g], r_t[g], jnp.where(eye, p_end[g], 0.0)), bd(s_cat[g])) for g in groups]
    av = [_mm(cat0(jnp.where(strict, a_all[g][:c, n:], 0.0), jnp.where(incl, a_all[g][c:, n:], 0.0),
                   k_ht[g]), bd(v[g])) for g in groups]

    p = [jnp.where(base, a_ab[g], 0.0) for g in groups]
    inv = [jnp.where(eye, 1.0, 0.0) + p[g] for g in groups]
    p = [_mm(p[g], bd(p[g])) for g in groups]
    for _ in range(BASE_BLOCK.bit_length() - 3):
        both = [_mm(cat0(p[g], inv[g]), bd(p[g])) for g in groups]
        p = [both[g][:c] for g in groups]
        inv = [inv[g] + both[g][c:] for g in groups]
    inv = [inv[g] + _mm(inv[g], bd(p[g])) for g in groups]
    for off in offs:
        x = [_mm(jnp.where(off, a_ab[g], 0.0), bd(inv[g])) for g in groups]
        inv = [inv[g] + _mm(inv[g], bd(x[g])) for g in groups]

    u = [_mm(inv[g], bd(xs[g][:c] + av[g][:c])) for g in groups]
    yu = [_mm(cat0(jnp.where(incl, a_all[g][c:, :n], 0.0), b_ht[g]), bd(u[g])) for g in groups]
    y = [xs[g][c:2 * c] + yu[g][:c] + av[g][c:2 * c] for g in groups]
    s_new = [xs[g][2 * c:] + yu[g][c:] + av[g][2 * c:] for g in groups]
    return y, s_new


def _rwkv_scan_kernel(r_ref, k_ref, v_ref, lw_ref, kk_ref, a_ref, y_ref, s_ref):
    @pl.when(pl.program_id(1) == 0)
    def _():
        s_ref[...] = jnp.zeros_like(s_ref)

    nb, _, d = r_ref.shape
    where = [(bi, slice(lo, lo + GROUP_LANES)) for bi in range(nb) for lo in range(0, d, GROUP_LANES)]
    load = lambda ref: [ref[bi, :, sl] for bi, sl in where]
    y, s_new = _scan_groups(load(r_ref), load(k_ref), load(v_ref), load(lw_ref), load(kk_ref),
                            load(a_ref), [s_ref[g] for g in range(len(where))], _scan_masks())
    for g, (bi, sl) in enumerate(where):
        y_ref[bi, :, sl] = y[g]
        s_ref[g] = s_new[g]


def _rwkv_scan(r, k, v, lw, kk, a, *, batch_per_step=4):
    b, t, d = r.shape
    nb = batch_per_step
    tile = pl.BlockSpec((nb, CHUNK, d), lambda i, j: (i, j, 0))
    return pl.pallas_call(
        _rwkv_scan_kernel,
        out_shape=jax.ShapeDtypeStruct((b, t, d), F32),
        grid=(b // nb, t // CHUNK),
        in_specs=[tile] * 6,
        out_specs=tile,
        scratch_shapes=[pltpu.VMEM((nb * d // GROUP_LANES, HEAD, GROUP_LANES), F32)],
        compiler_params=pltpu.CompilerParams(
            dimension_semantics=("parallel", "arbitrary"), vmem_limit_bytes=VMEM_LIMIT),
        name="rwkv_scan",
    )(r, k, v, lw, kk, a)


def _attn_kernel(sinks_ref, h_ref, kvtp_ref, kvtc_ref, gpre_ref, wqt_ref, bqt_ref,
                 wo_ref, bo_ref, gpost_ref, o_ref):
    step = pl.program_id(1)
    w = WINDOW
    h = h_ref[0]
    u = _rms(h, gpre_ref[...])
    qt = ((_mm_nt(wqt_ref[...], u) + bqt_ref[...]) * (LOG2E * HEAD ** -0.5)).astype(BF16)
    kvt_cur, kvt_prev = kvtc_ref[0], kvtp_ref[0]
    kvw = kvt_cur.shape[0] // 2
    lanes = GROUP * w

    def band(n):
        return (jnp.concatenate([kvt_prev, kvt_cur[:, :w]], axis=1) if n == 0
                else kvt_cur[:, (n - 1) * w:(n + 1) * w])

    si = lax.broadcasted_iota(jnp.int32, (2 * w, lanes), 0)
    qi = lax.broadcasted_iota(jnp.int32, (2 * w, lanes), 1) % w
    valid = (si > qi) & (si <= qi + w)
    valid_first = valid & ((si >= w) | (step > 0))
    seg = lax.broadcasted_iota(jnp.int32, (1, lanes), 1) // w

    def scores(n):
        kvt_band = band(n)
        out = []
        for kh in range(KV_HEADS):
            q_stack = jnp.concatenate(
                [qt[(kh * GROUP + g) * HEAD:(kh * GROUP + g + 1) * HEAD, n * w:(n + 1) * w]
                 for g in range(GROUP)], axis=1)
            out.append(lax.dot_general(kvt_band[kh * HEAD:(kh + 1) * HEAD], q_stack,
                                       (((0,), (0,)), ((), ())), preferred_element_type=F32))
        return out

    def softmax(n, st):
        mask = valid_first if n == 0 else valid
        out = []
        for kh in range(KV_HEADS):
            sink = sinks_ref[kh * GROUP] * LOG2E
            for g in range(1, GROUP):
                sink = jnp.where(seg == g, sinks_ref[kh * GROUP + g] * LOG2E, sink)
            s = jnp.where(mask, st[kh], MASK_VALUE)
            mx = jnp.maximum(jnp.max(s, axis=0, keepdims=True), sink)
            p = jnp.exp2(s - mx)
            denom = jnp.sum(p, axis=0, keepdims=True) + jnp.exp2(sink - mx)
            out.append((p.astype(BF16), 1.0 / denom))
        return out

    def values(n, pd):
        vt_band = band(n)[kvw:]
        heads = []
        for kh in range(KV_HEADS):
            p, inv_denom = pd[kh]
            ot = jnp.dot(vt_band[kh * HEAD:(kh + 1) * HEAD], p, preferred_element_type=F32) * inv_denom
            heads += [ot[:, g * w:(g + 1) * w] for g in range(GROUP)]
        return jnp.concatenate(heads, axis=0).astype(BF16)

    st = [scores(0)]
    cols = []
    for n in range(ATTN_BLOCKS):
        if n + 1 < ATTN_BLOCKS:
            st.append(scores(n + 1))
        cols.append(values(n, softmax(n, st[n])))
    ot_all = jnp.concatenate(cols, axis=1)
    m = lax.dot_general(ot_all, wo_ref[...], (((0,), (0,)), ((), ())),
                        preferred_element_type=F32) + bo_ref[...]
    o_ref[0] = h + _rms(m, gpost_ref[...])


def _attn_block(h3, kvt3, sinks, g_pre, w_qt, b_qt, w_o, b_o, g_post):
    b, t, d = h3.shape
    kv2 = kvt3.shape[1]
    tq = ATTN_BLOCKS * WINDOW
    tile = pl.BlockSpec((1, tq, d), lambda i, j: (i, j, 0))
    prev = lambda j: jnp.maximum(j * ATTN_BLOCKS - 1, 0)
    params = (g_pre, w_qt, b_qt, w_o, b_o, g_post)
    return pl.pallas_call(
        _attn_kernel,
        out_shape=jax.ShapeDtypeStruct((b, t, d), F32),
        grid=(b, t // tq),
        in_specs=[
            pl.BlockSpec(memory_space=pltpu.SMEM),
            tile,
            pl.BlockSpec((1, kv2, WINDOW), lambda i, j: (i, 0, prev(j))),
            pl.BlockSpec((1, kv2, tq), lambda i, j: (i, 0, j)),
        ] + [_full_spec(p) for p in params],
        out_specs=tile,
        compiler_params=pltpu.CompilerParams(
            dimension_semantics=("parallel", "parallel"), vmem_limit_bytes=VMEM_LIMIT),
        name="swa_block",
    )(sinks, h3, kvt3, kvt3, *params)


def kernel(x, norm_g, ffn_w_in, ffn_w_out, rwkv_mu, rwkv_w_rkv, rwkv_w_o, rwkv_w0, rwkv_w1, rwkv_w2, rwkv_a0, rwkv_a1, rwkv_a2, rwkv_g1, rwkv_g2, rwkv_k_k, rwkv_k_a, rwkv_r_k, rwkv_gn_g, rwkv_gn_b, kv_norm_g, w_kv, b_kv, attn_w_q, attn_b_q, attn_w_o, attn_b_o, attn_sinks):
    b, t, d = x.shape
    m = b * t
    depth = norm_g.shape[0]
    n_a = rwkv_mu.shape[0]
    row = lambda vec: vec.reshape(1, -1).astype(F32)
    col = lambda vec: vec.reshape(-1, 1).astype(F32)
    bf = lambda w: w.astype(BF16)

    lane_head = jnp.arange(GROUP_LANES) // HEAD
    ones_bd = (lane_head[:, None] == lane_head[None, :]).astype(BF16)

    h = x.reshape(m, d)
    kvt_sh = None
    w_in, w_out = ffn_w_in.astype(F32), ffn_w_out.astype(F32)
    kv_params = (row(kv_norm_g), bf(w_kv.T), col(b_kv))
    for layer in range(depth):
        g = norm_g[layer]
        h = _ffn_block(h, row(g[0]), w_in, w_out, row(g[1]), layer, 0)
        mix = None
        if layer < n_a:
            i = layer
            params = (row(g[2]), rwkv_mu[i], bf(rwkv_w_rkv[i]),
                      row(rwkv_w0[i]), bf(rwkv_w1[i]), bf(rwkv_w2[i]),
                      row(rwkv_a0[i]), bf(rwkv_a1[i]), bf(rwkv_a2[i]), bf(rwkv_g1[i]), bf(rwkv_g2[i]),
                      row(rwkv_k_k[i]), row(rwkv_k_a[i]), row(rwkv_r_k[i]), ones_bd)
            r, k, v, lw, kk, a, gate, bonus = _rwkv_prep(h.reshape(b, t, d), params)
            y = _rwkv_scan(r, k, v, lw, kk, a)
            mix = (y.reshape(m, d), bonus.reshape(m, d), gate.reshape(m, d), row(rwkv_gn_g[i]),
                   row(rwkv_gn_b[i]), bf(rwkv_w_o[i]), row(g[3]), ones_bd)
        else:
            j = layer - n_a
            h = _attn_block(h.reshape(b, t, d), kvt_sh, attn_sinks[j].astype(F32), row(g[2]),
                            bf(attn_w_q[j].T), col(attn_b_q[j]), bf(attn_w_o[j]), row(attn_b_o[j]),
                            row(g[3])).reshape(m, d)
        if layer == n_a - 1:
            h, kvt_sh = _ffn_block(h, row(g[4]), w_in, w_out, row(g[5]), layer, 1, mix=mix,
                                   kv=kv_params, seq_len=t)
        else:
            h = _ffn_block(h, row(g[4]), w_in, w_out, row(g[5]), layer, 1, mix=mix)
    return h.reshape(b, t, d)
```

```python
import functools

import jax
import jax.numpy as jnp
from jax import lax
from jax.experimental import pallas as pl
from jax.experimental.pallas import tpu as pltpu

F32 = jnp.float32
BF16 = jnp.bfloat16

RMS_EPS = 1e-6
GN_EPS = 64e-5
HEAD = 64
WINDOW = 128
MASK_VALUE = -1e30
KV_HEADS = 4
GROUP = 4
ATTN_BLOCKS = 4
LOG2E = 1.4426950408889634

CHUNK = 64
GROUP_LANES = 256
HEADS_PER_GROUP = GROUP_LANES // HEAD
BASE_BLOCK = 8
PREP_ROWS = 512
FFN_ROWS = 1024
FFN_TILE = 1024
FFN_MIX_TILE = 512
FFN_WEIGHT_STEPS = 11
EXP_MINUS_HALF = 0.6065306597126334

VMEM_LIMIT = 56 * 1024 * 1024
MXU_WIDTH = 256


def _rms(x, g):
    return x * lax.rsqrt(jnp.mean(x * x, axis=-1, keepdims=True) + RMS_EPS) * g


def _mm(a, b):
    return jnp.dot(a.astype(BF16), b.astype(BF16), preferred_element_type=F32)


def _mm_nt(a, b):
    return lax.dot_general(a.astype(BF16), b.astype(BF16), (((1,), (1,)), ((), ())),
                           preferred_element_type=F32)


def _split2(x):
    hi = x.astype(BF16)
    lo = (x - hi.astype(F32)).astype(BF16)
    return hi, lo


def _split3(x):
    h1 = x.astype(BF16)
    r1 = x - h1.astype(F32)
    h2 = r1.astype(BF16)
    h3 = (r1 - h2.astype(F32)).astype(BF16)
    return h1, h2, h3


def _seg_sum(x, ones_bd, pieces=1):
    m, d = x.shape
    xs = jnp.concatenate([x[:, lo:lo + GROUP_LANES] for lo in range(0, d, GROUP_LANES)], axis=0)
    parts = (xs.astype(BF16),) if pieces == 1 else _split2(xs)
    s = sum(jnp.dot(p, ones_bd, preferred_element_type=F32) for p in parts)
    return jnp.concatenate([s[i * m:(i + 1) * m] for i in range(d // GROUP_LANES)], axis=1)


def _full_spec(x):
    return pl.BlockSpec(x.shape, lambda *_: (0,) * x.ndim)


def _ffn_chunks(d_ff):
    tiles = d_ff // MXU_WIDTH
    first = (tiles + 1) // 2 * MXU_WIDTH
    return ((0, first), (first, d_ff))


def _ffn_kernel(*refs, has_mix, has_kv):
    refs = list(refs)
    h_ref = refs.pop(0)
    mix_refs = [refs.pop(0) for _ in range(8)] if has_mix else None
    gpre_ref, win_chunk_ref, wo_chunk_ref, gpost_ref = [refs.pop(0) for _ in range(4)]
    kv_refs = [refs.pop(0) for _ in range(3)] if has_kv else None
    o_ref = refs.pop(0)
    kv_out_refs = [refs.pop(0)] if has_kv else []
    win_ref, wo_ref = refs[:2]
    hmix_ref = refs[2] if has_mix else None
    d_ff = wo_ref.shape[0]
    step = pl.program_id(0)
    nw = FFN_WEIGHT_STEPS

    cw, cr = win_chunk_ref.shape[1], wo_chunk_ref.shape[0]
    for c in range(nw):
        @pl.when(step == c)
        def _(c=c):
            win_ref[:, c * cw:(c + 1) * cw] = win_chunk_ref[...].astype(BF16)
            wo_ref[c * cr:(c + 1) * cr, :] = wo_chunk_ref[...].astype(BF16)

    if has_mix:
        @pl.when(step == nw - 1)
        def _():
            stages = _mix_stages(h_ref, mix_refs)
            for stage in stages[:-1]:
                stage()
            hmix_ref[0] = stages[-1]()

    @pl.when(step >= nw)
    def _():
        if has_mix:
            slot = (step - nw) % 2
            stages = _mix_stages(h_ref, mix_refs)
            h_next = _ffn_rows(hmix_ref[slot], gpre_ref, win_ref, wo_ref, gpost_ref, kv_refs, o_ref,
                               kv_out_refs, d_ff, slice(None), stages)
            hmix_ref[1 - slot] = h_next
        else:
            sub = min(FFN_ROWS, h_ref.shape[0])
            for r0 in range(0, h_ref.shape[0], sub):
                rows = slice(r0, r0 + sub)
                _ffn_rows(h_ref[rows, :], gpre_ref, win_ref, wo_ref, gpost_ref, kv_refs, o_ref,
                          kv_out_refs, d_ff, rows, None)


def _mix_stages(h_ref, mix_refs):
    y_ref, bonus_ref, gate_ref, gng_ref, gnb_ref, wmix_ref, gmix_ref, ones_ref = mix_refs
    val = {}

    def mean():
        val["dev"] = y_ref[...] - _seg_sum(y_ref[...], ones_ref[...], pieces=2) * (1.0 / HEAD)

    def variance():
        val["var"] = _seg_sum(val["dev"] * val["dev"], ones_ref[...]) * (1.0 / HEAD)

    def project():
        yn = val["dev"] * lax.rsqrt(val["var"] + GN_EPS) * gng_ref[...] + gnb_ref[...] + bonus_ref[...]
        val["m"] = _mm(yn * gate_ref[...], wmix_ref[...])

    def residual():
        return h_ref[...] + _rms(val["m"], gmix_ref[...])

    return [mean, variance, project, residual]


def _ffn_rows(h, gpre_ref, win_ref, wo_ref, gpost_ref, kv_refs, o_ref, kv_out_refs, d_ff, rows, side):
    side = list(side) if side else []
    run_side = lambda: side.pop(0)() if len(side) > 1 else None
    run_side()
    xn = _rms(h, gpre_ref[...]).astype(BF16)
    acc = None
    for lo, hi in _ffn_chunks(d_ff):
        gate = jnp.dot(xn, win_ref[:, lo:hi], preferred_element_type=F32)
        up = jnp.dot(xn, win_ref[:, d_ff + lo:d_ff + hi], preferred_element_type=F32)
        run_side()
        act = (gate * jax.nn.sigmoid(gate) * up).astype(BF16)
        part = jnp.dot(act, wo_ref[lo:hi, :], preferred_element_type=F32)
        acc = part if acc is None else acc + part
    while len(side) > 1:
        run_side()
    h = h + 0.5 * _rms(acc, gpost_ref[...])
    o_ref[rows, :] = h
    if kv_refs is not None:
        gkv_ref, wkvt_ref, bkvt_ref = kv_refs
        (kvt_ref,) = kv_out_refs
        u = _rms(h, gkv_ref[...]).astype(BF16)
        kvt_ref[0, :, rows] = (_mm_nt(wkvt_ref[...], u) + bkvt_ref[...]).astype(BF16)
    return side[0]() if side else None


def _ffn_block(h, g_pre, w_in, w_out, g_post, layer, which, *, mix=None, kv=None, seq_len=None):
    m, d = h.shape
    d_ff = w_out.shape[2]
    nw = FFN_WEIGHT_STEPS
    tm = FFN_MIX_TILE if mix is not None else FFN_TILE
    row_tile = lambda i: jnp.maximum(i - nw, 0)
    chunk = lambda i: jnp.minimum(i, nw - 1)
    tile = pl.BlockSpec((tm, d), lambda i: (row_tile(i), 0))
    resident = lambda x: pl.BlockSpec(x.shape, lambda i: (0,) * x.ndim, pipeline_mode=pl.Buffered(1))
    scratch = [pltpu.VMEM((d, 2 * d_ff), BF16), pltpu.VMEM((d_ff, d), BF16)]
    if mix is None:
        args, in_specs = [h], [tile]
    else:
        ahead = pl.BlockSpec((tm, d), lambda i: (jnp.clip(i - nw + 1, 0, m // tm - 1), 0))
        args = [h] + list(mix)
        in_specs = [ahead] * 4 + [resident(p) for p in mix[3:]]
        scratch.append(pltpu.VMEM((2, tm, d), F32))
    args += [g_pre, w_in, w_out, g_post]
    in_specs += [resident(g_pre),
                 pl.BlockSpec((None, None, d, 2 * d_ff // nw), lambda i: (layer, which, 0, chunk(i))),
                 pl.BlockSpec((None, None, d_ff // nw, d), lambda i: (layer, which, chunk(i), 0)),
                 resident(g_post)]
    out_shape, out_specs = [jax.ShapeDtypeStruct((m, d), F32)], [tile]
    if kv is not None:
        args += list(kv)
        in_specs += [resident(p) for p in kv]
        kv2 = kv[1].shape[0]
        steps = seq_len // tm
        out_shape += [jax.ShapeDtypeStruct((m // seq_len, kv2, seq_len), BF16)]
        out_specs += [pl.BlockSpec((1, kv2, tm), lambda i: (row_tile(i) // steps, 0, row_tile(i) % steps))]
    out = pl.pallas_call(
        functools.partial(_ffn_kernel, has_mix=mix is not None, has_kv=kv is not None),
        out_shape=out_shape,
        grid=(nw + m // tm,),
        in_specs=in_specs,
        out_specs=out_specs,
        scratch_shapes=scratch,
        compiler_params=pltpu.CompilerParams(
            dimension_semantics=("arbitrary",), vmem_limit_bytes=VMEM_LIMIT),
        name="ffn_block",
    )(*args)
    return out if kv is not None else out[0]


def _rwkv_prep_kernel(h_ref, g_ref, mu_ref, wrkv_ref, w0_ref, w1_ref, w2_ref,
                      a0_ref, a1_ref, a2_ref, g1_ref, g2_ref, kk_ref, ka_ref, rk_ref,
                      ones_ref,
                      r_out, k_out, v_out, lw_out, kk_out, a_out, g_out, bonus_out,
                      carry_ref):
    @pl.when(pl.program_id(1) == 0)
    def _():
        carry_ref[...] = jnp.zeros_like(carry_ref)

    ones_bd = ones_ref[...]
    mu = mu_ref[...]
    tm = h_ref.shape[1]
    row = lax.broadcasted_iota(jnp.int32, (PREP_ROWS, h_ref.shape[2]), 0)
    last = carry_ref[0:1, :]
    for r0 in range(0, tm, PREP_ROWS):
        rows = slice(r0, r0 + PREP_ROWS)
        u = _rms(h_ref[0, rows, :], g_ref[...])
        prev = jnp.where(row == 0, last, pltpu.roll(u, shift=1, axis=0))
        last = u[PREP_ROWS - 1:PREP_ROWS, :]
        xx = prev - u
        xr, xw, xk, xv, xa, xg = [u + xx * mu[i:i + 1, :] for i in range(6)]

        w_mid = jnp.tanh(_mm(xw, w1_ref[...]))
        a_mid = _mm(xa, a1_ref[...])
        g_mid = jax.nn.sigmoid(_mm(xg, g1_ref[...]))
        r = _mm(xr, wrkv_ref[0])
        k = _mm(xk, wrkv_ref[1])
        v = _mm(xv, wrkv_ref[2])
        lw = jax.nn.sigmoid(w0_ref[...] + _mm(w_mid, w2_ref[...])) * (-EXP_MINUS_HALF)
        a = jax.nn.sigmoid(a0_ref[...] + _mm(a_mid, a2_ref[...]))
        g = _mm(g_mid, g2_ref[...])

        kk = k * kk_ref[...]
        norm = jnp.sqrt(_seg_sum(kk * kk, ones_bd))
        kk = kk / jnp.maximum(norm, 1e-12)
        k = k * (1.0 + (a - 1.0) * ka_ref[...])

        r_out[0, rows, :] = r
        k_out[0, rows, :] = k
        v_out[0, rows, :] = v.astype(BF16)
        lw_out[0, rows, :] = lw
        kk_out[0, rows, :] = kk
        a_out[0, rows, :] = a
        g_out[0, rows, :] = g.astype(BF16)
        bonus_out[0, rows, :] = (_seg_sum(r * k * rk_ref[...], ones_bd) * v).astype(BF16)
    carry_ref[0:1, :] = last


def _rwkv_prep(h3, params, *, tm=512):
    b, t, d = h3.shape
    tile = pl.BlockSpec((1, tm, d), lambda i, j: (i, j, 0))
    resident = lambda x: pl.BlockSpec(x.shape, lambda i, j: (0,) * x.ndim, pipeline_mode=pl.Buffered(1))
    dtypes = [F32, F32, BF16, F32, F32, F32, BF16, BF16]
    return pl.pallas_call(
        _rwkv_prep_kernel,
        out_shape=[jax.ShapeDtypeStruct((b, t, d), dt) for dt in dtypes],
        grid=(b, t // tm),
        in_specs=[tile] + [resident(p) for p in params],
        out_specs=[tile] * 8,
        scratch_shapes=[pltpu.VMEM((8, d), F32)],
        compiler_params=pltpu.CompilerParams(
            dimension_semantics=("parallel", "arbitrary"), vmem_limit_bytes=VMEM_LIMIT),
        name="rwkv_prep",
    )(h3, *params)


def _scan_masks():
    c, n = CHUNK, GROUP_LANES
    row_s = lax.broadcasted_iota(jnp.int32, (c, n), 0)
    col_s = lax.broadcasted_iota(jnp.int32, (c, n), 1) % c
    strict = row_s > col_s
    incl = row_s >= col_s
    eye = row_s == col_s
    base = strict & ((row_s // BASE_BLOCK) == (col_s // BASE_BLOCK))
    offs = []
    b = BASE_BLOCK
    while b < c:
        offs.append(((row_s // (2 * b)) == (col_s // (2 * b)))
                    & ((row_s // b) % 2 == 1) & ((col_s // b) % 2 == 0))
        b *= 2
    row_b = lax.broadcasted_iota(jnp.int32, (n, n), 0)
    col_b = lax.broadcasted_iota(jnp.int32, (n, n), 1)
    mask_bd = (row_b // HEAD) == (col_b // HEAD)
    row_ge = {}
    shift = 1
    while shift < c:
        row_ge[shift] = row_s >= shift
        shift *= 2
    return mask_bd, strict, incl, eye, base, tuple(offs), row_ge


def _block_diag(z, mask_bd):
    tiled = jnp.concatenate([z] * HEADS_PER_GROUP, axis=0)
    return jnp.where(mask_bd, tiled, 0.0)


def _head_transpose(x):
    xt = x.T
    return jnp.concatenate([xt[h * HEAD:(h + 1) * HEAD, :] for h in range(HEADS_PER_GROUP)], axis=1)


def _scan_groups(r, k, v, lw, kk, a, s_cat, masks):
    mask_bd, strict, incl, eye, base, offs, row_ge = masks
    c, n = CHUNK, GROUP_LANES
    groups = range(len(r))
    bd = lambda z: _block_diag(z, mask_bd)
    cat0 = lambda *xs: jnp.concatenate(xs, axis=0)

    l_cum = list(lw)
    shift = 1
    while shift < c:
        l_cum = [l_cum[g] + jnp.where(row_ge[shift], pltpu.roll(l_cum[g], shift=shift, axis=0), 0.0)
                 for g in groups]
        shift *= 2
    e_l = [jnp.exp(l_cum[g]) for g in groups]
    e_nl = [jnp.exp(-l_cum[g]) for g in groups]
    a_t = [-kk[g] * jnp.exp(l_cum[g] - lw[g]) for g in groups]
    r_t = [r[g] * e_l[g] for g in groups]
    b_t = [kk[g] * a[g] * e_nl[g] for g in groups]
    k_t = [k[g] * e_nl[g] for g in groups]
    p_end = [e_l[g][c - 1:c, :] for g in groups]
    b_ht = [_head_transpose(b_t[g] * p_end[g]) for g in groups]
    k_ht = [_head_transpose(k_t[g] * p_end[g]) for g in groups]

    a_all = [_mm_nt(cat0(a_t[g], r_t[g]), cat0(bd(b_t[g]), bd(k_t[g]))) for g in groups]
    a_ab = [a_all[g][:c, :n] for g in groups]
    xs = [_mm(cat0(a_t[g], r_t[g], jnp.where(eye, p_end[g], 0.0)), bd(s_cat[g])) for g in groups]
    av = [_mm(cat0(jnp.where(strict, a_all[g][:c, n:], 0.0), jnp.where(incl, a_all[g][c:, n:], 0.0),
                   k_ht[g]), bd(v[g])) for g in groups]

    p = [jnp.where(base, a_ab[g], 0.0) for g in groups]
    inv = [jnp.where(eye, 1.0, 0.0) + p[g] for g in groups]
    p = [_mm(p[g], bd(p[g])) for g in groups]
    for _ in range(BASE_BLOCK.bit_length() - 3):
        both = [_mm(cat0(p[g], inv[g]), bd(p[g])) for g in groups]
        p = [both[g][:c] for g in groups]
        inv = [inv[g] + both[g][c:] for g in groups]
    inv = [inv[g] + _mm(inv[g], bd(p[g])) for g in groups]
    for off in offs:
        x = [_mm(jnp.where(off, a_ab[g], 0.0), bd(inv[g])) for g in groups]
        inv = [inv[g] + _mm(inv[g], bd(x[g])) for g in groups]

    u = [_mm(inv[g], bd(xs[g][:c] + av[g][:c])) for g in groups]
    yu = [_mm(cat0(jnp.where(incl, a_all[g][c:, :n], 0.0), b_ht[g]), bd(u[g])) for g in groups]
    y = [xs[g][c:2 * c] + yu[g][:c] + av[g][c:2 * c] for g in groups]
    s_new = [xs[g][2 * c:] + yu[g][c:] + av[g][2 * c:] for g in groups]
    return y, s_new


def _rwkv_scan_kernel(r_ref, k_ref, v_ref, lw_ref, kk_ref, a_ref, y_ref, s_ref):
    @pl.when(pl.program_id(1) == 0)
    def _():
        s_ref[...] = jnp.zeros_like(s_ref)

    nb, _, d = r_ref.shape
    where = [(bi, slice(lo, lo + GROUP_LANES)) for bi in range(nb) for lo in range(0, d, GROUP_LANES)]
    load = lambda ref: [ref[bi, :, sl] for bi, sl in where]
    y, s_new = _scan_groups(load(r_ref), load(k_ref), load(v_ref), load(lw_ref), load(kk_ref),
                            load(a_ref), [s_ref[g] for g in range(len(where))], _scan_masks())
    for g, (bi, sl) in enumerate(where):
        y_ref[bi, :, sl] = y[g]
        s_ref[g] = s_new[g]


def _rwkv_scan(r, k, v, lw, kk, a, *, batch_per_step=4):
    b, t, d = r.shape
    nb = batch_per_step
    tile = pl.BlockSpec((nb, CHUNK, d), lambda i, j: (i, j, 0))
    return pl.pallas_call(
        _rwkv_scan_kernel,
        out_shape=jax.ShapeDtypeStruct((b, t, d), F32),
        grid=(b // nb, t // CHUNK),
        in_specs=[tile] * 6,
        out_specs=tile,
        scratch_shapes=[pltpu.VMEM((nb * d // GROUP_LANES, HEAD, GROUP_LANES), F32)],
        compiler_params=pltpu.CompilerParams(
            dimension_semantics=("parallel", "arbitrary"), vmem_limit_bytes=VMEM_LIMIT),
        name="rwkv_scan",
    )(r, k, v, lw, kk, a)


def _attn_kernel(sinks_ref, h_ref, kvtp_ref, kvtc_ref, gpre_ref, wqt_ref, bqt_ref,
                 wo_ref, bo_ref, gpost_ref, o_ref):
    step = pl.program_id(1)
    w = WINDOW
    h = h_ref[0]
    u = _rms(h, gpre_ref[...])
    qt = ((_mm_nt(wqt_ref[...], u) + bqt_ref[...]) * (LOG2E * HEAD ** -0.5)).astype(BF16)
    kvt_cur, kvt_prev = kvtc_ref[0], kvtp_ref[0]
    kvw = kvt_cur.shape[0] // 2
    lanes = GROUP * w

    def band(n):
        return (jnp.concatenate([kvt_prev, kvt_cur[:, :w]], axis=1) if n == 0
                else kvt_cur[:, (n - 1) * w:(n + 1) * w])

    si = lax.broadcasted_iota(jnp.int32, (2 * w, lanes), 0)
    qi = lax.broadcasted_iota(jnp.int32, (2 * w, lanes), 1) % w
    valid = (si > qi) & (si <= qi + w)
    valid_first = valid & ((si >= w) | (step > 0))
    seg = lax.broadcasted_iota(jnp.int32, (1, lanes), 1) // w

    def scores(n):
        kvt_band = band(n)
        out = []
        for kh in range(KV_HEADS):
            q_stack = jnp.concatenate(
                [qt[(kh * GROUP + g) * HEAD:(kh * GROUP + g + 1) * HEAD, n * w:(n + 1) * w]
                 for g in range(GROUP)], axis=1)
            out.append(lax.dot_general(kvt_band[kh * HEAD:(kh + 1) * HEAD], q_stack,
                                       (((0,), (0,)), ((), ())), preferred_element_type=F32))
        return out

    def softmax(n, st):
        mask = valid_first if n == 0 else valid
        out = []
        for kh in range(KV_HEADS):
            sink = sinks_ref[kh * GROUP] * LOG2E
            for g in range(1, GROUP):
                sink = jnp.where(seg == g, sinks_ref[kh * GROUP + g] * LOG2E, sink)
            s = jnp.where(mask, st[kh], MASK_VALUE)
            mx = jnp.maximum(jnp.max(s, axis=0, keepdims=True), sink)
            p = jnp.exp2(s - mx)
            denom = jnp.sum(p, axis=0, keepdims=True) + jnp.exp2(sink - mx)
            out.append((p.astype(BF16), 1.0 / denom))
        return out

    def values(n, pd):
        vt_band = band(n)[kvw:]
        heads = []
        for kh in range(KV_HEADS):
            p, inv_denom = pd[kh]
            ot = jnp.dot(vt_band[kh * HEAD:(kh + 1) * HEAD], p, preferred_element_type=F32) * inv_denom
            heads += [ot[:, g * w:(g + 1) * w] for g in range(GROUP)]
        return jnp.concatenate(heads, axis=0).astype(BF16)

    st = [scores(0)]
    cols = []
    for n in range(ATTN_BLOCKS):
        if n + 1 < ATTN_BLOCKS:
            st.append(scores(n + 1))
        cols.append(values(n, softmax(n, st[n])))
    ot_all = jnp.concatenate(cols, axis=1)
    m = lax.dot_general(ot_all, wo_ref[...], (((0,), (0,)), ((), ())),
                        preferred_element_type=F32) + bo_ref[...]
    o_ref[0] = h + _rms(m, gpost_ref[...])


def _attn_block(h3, kvt3, sinks, g_pre, w_qt, b_qt, w_o, b_o, g_post):
    b, t, d = h3.shape
    kv2 = kvt3.shape[1]
    tq = ATTN_BLOCKS * WINDOW
    tile = pl.BlockSpec((1, tq, d), lambda i, j: (i, j, 0))
    prev = lambda j: jnp.maximum(j * ATTN_BLOCKS - 1, 0)
    params = (g_pre, w_qt, b_qt, w_o, b_o, g_post)
    return pl.pallas_call(
        _attn_kernel,
        out_shape=jax.ShapeDtypeStruct((b, t, d), F32),
        grid=(b, t // tq),
        in_specs=[
            pl.BlockSpec(memory_space=pltpu.SMEM),
            tile,
            pl.BlockSpec((1, kv2, WINDOW), lambda i, j: (i, 0, prev(j))),
            pl.BlockSpec((1, kv2, tq), lambda i, j: (i, 0, j)),
        ] + [_full_spec(p) for p in params],
        out_specs=tile,
        compiler_params=pltpu.CompilerParams(
            dimension_semantics=("parallel", "parallel"), vmem_limit_bytes=VMEM_LIMIT),
        name="swa_block",
    )(sinks, h3, kvt3, kvt3, *params)


def kernel(x, norm_g, ffn_w_in, ffn_w_out, rwkv_mu, rwkv_w_rkv, rwkv_w_o, rwkv_w0, rwkv_w1, rwkv_w2, rwkv_a0, rwkv_a1, rwkv_a2, rwkv_g1, rwkv_g2, rwkv_k_k, rwkv_k_a, rwkv_r_k, rwkv_gn_g, rwkv_gn_b, kv_norm_g, w_kv, b_kv, attn_w_q, attn_b_q, attn_w_o, attn_b_o, attn_sinks):
    b, t, d = x.shape
    m = b * t
    depth = norm_g.shape[0]
    n_a = rwkv_mu.shape[0]
    row = lambda vec: vec.reshape(1, -1).astype(F32)
    col = lambda vec: vec.reshape(-1, 1).astype(F32)
    bf = lambda w: w.astype(BF16)

    lane_head = jnp.arange(GROUP_LANES) // HEAD
    ones_bd = (lane_head[:, None] == lane_head[None, :]).astype(BF16)

    h = x.reshape(m, d)
    kvt_sh = None
    w_in, w_out = ffn_w_in.astype(F32), ffn_w_out.astype(F32)
    kv_params = (row(kv_norm_g), bf(w_kv.T), col(b_kv))
    for layer in range(depth):
        g = norm_g[layer]
        h = _ffn_block(h, row(g[0]), w_in, w_out, row(g[1]), layer, 0)
        mix = None
        if layer < n_a:
            i = layer
            params = (row(g[2]), rwkv_mu[i], bf(rwkv_w_rkv[i]),
                      row(rwkv_w0[i]), bf(rwkv_w1[i]), bf(rwkv_w2[i]),
                      row(rwkv_a0[i]), bf(rwkv_a1[i]), bf(rwkv_a2[i]), bf(rwkv_g1[i]), bf(rwkv_g2[i]),
                      row(rwkv_k_k[i]), row(rwkv_k_a[i]), row(rwkv_r_k[i]), ones_bd)
            r, k, v, lw, kk, a, gate, bonus = _rwkv_prep(h.reshape(b, t, d), params)
            y = _rwkv_scan(r, k, v, lw, kk, a)
            mix = (y.reshape(m, d), bonus.reshape(m, d), gate.reshape(m, d), row(rwkv_gn_g[i]),
                   row(rwkv_gn_b[i]), bf(rwkv_w_o[i]), row(g[3]), ones_bd)
        else:
            j = layer - n_a
            h = _attn_block(h.reshape(b, t, d), kvt_sh, attn_sinks[j].astype(F32), row(g[2]),
                            bf(attn_w_q[j].T), col(attn_b_q[j]), bf(attn_w_o[j]), row(attn_b_o[j]),
                            row(g[3])).reshape(m, d)
        if layer == n_a - 1:
            h, kvt_sh = _ffn_block(h, row(g[4]), w_in, w_out, row(g[5]), layer, 1, mix=mix,
                                   kv=kv_params, seq_len=t)
        else:
            h = _ffn_block(h, row(g[4]), w_in, w_out, row(g[5]), layer, 1, mix=mix)
    return h.reshape(b, t, d)
```

```python
import functools

import jax
import jax.numpy as jnp
from jax import lax
from jax.experimental import pallas as pl
from jax.experimental.pallas import tpu as pltpu

F32 = jnp.float32
BF16 = jnp.bfloat16

RMS_EPS = 1e-6
GN_EPS = 64e-5
HEAD = 64
WINDOW = 128
MASK_VALUE = -1e30
KV_HEADS = 4
GROUP = 4
ATTN_BLOCKS = 4
LOG2E = 1.4426950408889634

CHUNK = 64
GROUP_LANES = 256
HEADS_PER_GROUP = GROUP_LANES // HEAD
BASE_BLOCK = 8
PREP_ROWS = 512
FFN_ROWS = 1024
FFN_TILE = 1024
FFN_MIX_TILE = 512
FFN_WEIGHT_STEPS = 11
EXP_MINUS_HALF = 0.6065306597126334

VMEM_LIMIT = 56 * 1024 * 1024
MXU_WIDTH = 256


def _rms(x, g):
    return x * lax.rsqrt(jnp.mean(x * x, axis=-1, keepdims=True) + RMS_EPS) * g


def _mm(a, b):
    return jnp.dot(a.astype(BF16), b.astype(BF16), preferred_element_type=F32)


def _mm_nt(a, b):
    return lax.dot_general(a.astype(BF16), b.astype(BF16), (((1,), (1,)), ((), ())),
                           preferred_element_type=F32)


def _split2(x):
    hi = x.astype(BF16)
    lo = (x - hi.astype(F32)).astype(BF16)
    return hi, lo


def _split3(x):
    h1 = x.astype(BF16)
    r1 = x - h1.astype(F32)
    h2 = r1.astype(BF16)
    h3 = (r1 - h2.astype(F32)).astype(BF16)
    return h1, h2, h3


def _seg_sum(x, ones_bd, pieces=1):
    m, d = x.shape
    xs = jnp.concatenate([x[:, lo:lo + GROUP_LANES] for lo in range(0, d, GROUP_LANES)], axis=0)
    parts = (xs.astype(BF16),) if pieces == 1 else _split2(xs)
    s = sum(jnp.dot(p, ones_bd, preferred_element_type=F32) for p in parts)
    return jnp.concatenate([s[i * m:(i + 1) * m] for i in range(d // GROUP_LANES)], axis=1)


def _full_spec(x):
    return pl.BlockSpec(x.shape, lambda *_: (0,) * x.ndim)


def _ffn_chunks(d_ff):
    tiles = d_ff // MXU_WIDTH
    first = (tiles + 1) // 2 * MXU_WIDTH
    return ((0, first), (first, d_ff))


def _ffn_kernel(*refs, has_mix, has_kv):
    refs = list(refs)
    h_ref = refs.pop(0)
    mix_refs = [refs.pop(0) for _ in range(8)] if has_mix else None
    gpre_ref, wg_chunk_ref, wu_chunk_ref, wo_chunk_ref, gpost_ref = [refs.pop(0) for _ in range(5)]
    kv_refs = [refs.pop(0) for _ in range(3)] if has_kv else None
    o_ref = refs.pop(0)
    kv_out_refs = [refs.pop(0)] if has_kv else []
    win_ref, wo_ref = refs[:2]
    hmix_ref = refs[2] if has_mix else None
    xn0_ref = None if has_mix else refs[2]
    d_ff = wo_ref.shape[0]
    step = pl.program_id(0)
    nw = FFN_WEIGHT_STEPS

    cw = wg_chunk_ref.shape[1]
    for c in range(nw):
        @pl.when(step == c)
        def _(c=c):
            wg, wu, wo = [r[...].astype(BF16) for r in (wg_chunk_ref, wu_chunk_ref, wo_chunk_ref)]
            win_ref[:, c * cw:(c + 1) * cw] = wg
            win_ref[:, d_ff + c * cw:d_ff + (c + 1) * cw] = wu
            wo_ref[c * cw:(c + 1) * cw, :] = wo
            if not has_mix:
                if c == 0:
                    xn0_ref[...] = _rms(h_ref[...], gpre_ref[...]).astype(BF16)
                xn = xn0_ref[...]
                gate = jnp.dot(xn, wg, preferred_element_type=F32)
                up = jnp.dot(xn, wu, preferred_element_type=F32)
                part = jnp.dot((gate * jax.nn.sigmoid(gate) * up).astype(BF16), wo, preferred_element_type=F32)
                acc = part if c == 0 else o_ref[...] + part
                o_ref[...] = acc if c < nw - 1 else h_ref[...] + 0.5 * _rms(acc, gpost_ref[...])

    if has_mix:
        @pl.when(step == nw - 1)
        def _():
            stages = _mix_stages(h_ref, mix_refs)
            for stage in stages[:-1]:
                stage()
            hmix_ref[0] = stages[-1]()

    @pl.when(step >= nw)
    def _():
        if has_mix:
            slot = (step - nw) % 2
            stages = _mix_stages(h_ref, mix_refs)
            h_next = _ffn_rows(hmix_ref[slot], gpre_ref, win_ref, wo_ref, gpost_ref, kv_refs, o_ref,
                               kv_out_refs, d_ff, slice(None), stages)
            hmix_ref[1 - slot] = h_next
        else:
            sub = min(FFN_ROWS, h_ref.shape[0])
            for r0 in range(0, h_ref.shape[0], sub):
                rows = slice(r0, r0 + sub)
                _ffn_rows(h_ref[rows, :], gpre_ref, win_ref, wo_ref, gpost_ref, kv_refs, o_ref,
                          kv_out_refs, d_ff, rows, None)


def _mix_stages(h_ref, mix_refs):
    y_ref, bonus_ref, gate_ref, gng_ref, gnb_ref, wmix_ref, gmix_ref, ones_ref = mix_refs
    val = {}

    def mean():
        val["dev"] = y_ref[...] - _seg_sum(y_ref[...], ones_ref[...], pieces=2) * (1.0 / HEAD)

    def variance():
        val["var"] = _seg_sum(val["dev"] * val["dev"], ones_ref[...]) * (1.0 / HEAD)

    def project():
        yn = val["dev"] * lax.rsqrt(val["var"] + GN_EPS) * gng_ref[...] + gnb_ref[...] + bonus_ref[...]
        val["m"] = _mm(yn * gate_ref[...], wmix_ref[...])

    def residual():
        return h_ref[...] + _rms(val["m"], gmix_ref[...])

    return [mean, variance, project, residual]


def _ffn_rows(h, gpre_ref, win_ref, wo_ref, gpost_ref, kv_refs, o_ref, kv_out_refs, d_ff, rows, side):
    side = list(side) if side else []
    run_side = lambda: side.pop(0)() if len(side) > 1 else None
    run_side()
    xn = _rms(h, gpre_ref[...]).astype(BF16)
    acc = None
    for lo, hi in _ffn_chunks(d_ff):
        gate = jnp.dot(xn, win_ref[:, lo:hi], preferred_element_type=F32)
        up = jnp.dot(xn, win_ref[:, d_ff + lo:d_ff + hi], preferred_element_type=F32)
        run_side()
        act = (gate * jax.nn.sigmoid(gate) * up).astype(BF16)
        part = jnp.dot(act, wo_ref[lo:hi, :], preferred_element_type=F32)
        acc = part if acc is None else acc + part
    while len(side) > 1:
        run_side()
    h = h + 0.5 * _rms(acc, gpost_ref[...])
    o_ref[rows, :] = h
    if kv_refs is not None:
        gkv_ref, wkvt_ref, bkvt_ref = kv_refs
        (kvt_ref,) = kv_out_refs
        u = _rms(h, gkv_ref[...]).astype(BF16)
        kvt_ref[0, :, rows] = (_mm_nt(wkvt_ref[...], u) + bkvt_ref[...]).astype(BF16)
    return side[0]() if side else None


def _ffn_block(h, g_pre, w_in, w_out, g_post, layer, which, *, mix=None, kv=None, seq_len=None):
    assert kv is None or mix is not None
    m, d = h.shape
    d_ff = w_out.shape[2]
    nw = FFN_WEIGHT_STEPS
    cw = d_ff // nw
    chunk = lambda i: jnp.minimum(i, nw - 1)
    resident = lambda x: pl.BlockSpec(x.shape, lambda i: (0,) * x.ndim, pipeline_mode=pl.Buffered(1))
    scratch = [pltpu.VMEM((d, 2 * d_ff), BF16), pltpu.VMEM((d_ff, d), BF16)]
    if mix is None:
        tm = FFN_TILE
        row_tile = lambda i: jnp.maximum(i - nw + 1, 0)
        n_steps = nw + m // tm - 1
        tile = pl.BlockSpec((tm, d), lambda i: (row_tile(i), 0))
        args, in_specs = [h], [tile]
        scratch.append(pltpu.VMEM((tm, d), BF16))
    else:
        tm = FFN_MIX_TILE
        row_tile = lambda i: jnp.maximum(i - nw, 0)
        n_steps = nw + m // tm
        tile = pl.BlockSpec((tm, d), lambda i: (row_tile(i), 0))
        ahead = pl.BlockSpec((tm, d), lambda i: (jnp.clip(i - nw + 1, 0, m // tm - 1), 0))
        args = [h] + list(mix)
        in_specs = [ahead] * 4 + [resident(p) for p in mix[3:]]
        scratch.append(pltpu.VMEM((2, tm, d), F32))
    args += [g_pre, w_in, w_in, w_out, g_post]
    in_specs += [resident(g_pre),
                 pl.BlockSpec((None, None, d, cw), lambda i: (layer, which, 0, chunk(i))),
                 pl.BlockSpec((None, None, d, cw), lambda i: (layer, which, 0, nw + chunk(i))),
                 pl.BlockSpec((None, None, cw, d), lambda i: (layer, which, chunk(i), 0)),
                 resident(g_post)]
    out_shape, out_specs = [jax.ShapeDtypeStruct((m, d), F32)], [tile]
    if kv is not None:
        args += list(kv)
        in_specs += [resident(p) for p in kv]
        kv2 = kv[1].shape[0]
        steps = seq_len // tm
        out_shape += [jax.ShapeDtypeStruct((m // seq_len, kv2, seq_len), BF16)]
        out_specs += [pl.BlockSpec((1, kv2, tm), lambda i: (row_tile(i) // steps, 0, row_tile(i) % steps))]
    out = pl.pallas_call(
        functools.partial(_ffn_kernel, has_mix=mix is not None, has_kv=kv is not None),
        out_shape=out_shape,
        grid=(n_steps,),
        in_specs=in_specs,
        out_specs=out_specs,
        scratch_shapes=scratch,
        compiler_params=pltpu.CompilerParams(
            dimension_semantics=("arbitrary",), vmem_limit_bytes=VMEM_LIMIT),
        name="ffn_block",
    )(*args)
    return out if kv is not None else out[0]


def _rwkv_prep_kernel(h_ref, g_ref, mu_ref, wrkv_ref, w0_ref, w1_ref, w2_ref,
                      a0_ref, a1_ref, a2_ref, g1_ref, g2_ref, kk_ref, ka_ref, rk_ref,
                      ones_ref,
                      r_out, k_out, v_out, lw_out, kk_out, a_out, g_out, bonus_out,
                      carry_ref):
    @pl.when(pl.program_id(1) == 0)
    def _():
        carry_ref[...] = jnp.zeros_like(carry_ref)

    ones_bd = ones_ref[...]
    mu = mu_ref[...]
    tm = h_ref.shape[1]
    row = lax.broadcasted_iota(jnp.int32, (PREP_ROWS, h_ref.shape[2]), 0)
    last = carry_ref[0:1, :]
    for r0 in range(0, tm, PREP_ROWS):
        rows = slice(r0, r0 + PREP_ROWS)
        u = _rms(h_ref[0, rows, :], g_ref[...])
        prev = jnp.where(row == 0, last, pltpu.roll(u, shift=1, axis=0))
        last = u[PREP_ROWS - 1:PREP_ROWS, :]
        xx = prev - u
        xr, xw, xk, xv, xa, xg = [u + xx * mu[i:i + 1, :] for i in range(6)]

        w_mid = jnp.tanh(_mm(xw, w1_ref[...]))
        a_mid = _mm(xa, a1_ref[...])
        g_mid = jax.nn.sigmoid(_mm(xg, g1_ref[...]))
        r = _mm(xr, wrkv_ref[0])
        k = _mm(xk, wrkv_ref[1])
        v = _mm(xv, wrkv_ref[2])
        lw = jax.nn.sigmoid(w0_ref[...] + _mm(w_mid, w2_ref[...])) * (-EXP_MINUS_HALF)
        a = jax.nn.sigmoid(a0_ref[...] + _mm(a_mid, a2_ref[...]))
        g = _mm(g_mid, g2_ref[...])

        kk = k * kk_ref[...]
        norm = jnp.sqrt(_seg_sum(kk * kk, ones_bd))
        kk = kk / jnp.maximum(norm, 1e-12)
        k = k * (1.0 + (a - 1.0) * ka_ref[...])

        r_out[0, rows, :] = r
        k_out[0, rows, :] = k
        v_out[0, rows, :] = v.astype(BF16)
        lw_out[0, rows, :] = lw
        kk_out[0, rows, :] = kk
        a_out[0, rows, :] = a
        g_out[0, rows, :] = g.astype(BF16)
        bonus_out[0, rows, :] = (_seg_sum(r * k * rk_ref[...], ones_bd) * v).astype(BF16)
    carry_ref[0:1, :] = last


def _rwkv_prep(h3, params, *, tm=512):
    b, t, d = h3.shape
    tile = pl.BlockSpec((1, tm, d), lambda i, j: (i, j, 0))
    resident = lambda x: pl.BlockSpec(x.shape, lambda i, j: (0,) * x.ndim, pipeline_mode=pl.Buffered(1))
    dtypes = [F32, F32, BF16, F32, F32, F32, BF16, BF16]
    return pl.pallas_call(
        _rwkv_prep_kernel,
        out_shape=[jax.ShapeDtypeStruct((b, t, d), dt) for dt in dtypes],
        grid=(b, t // tm),
        in_specs=[tile] + [resident(p) for p in params],
        out_specs=[tile] * 8,
        scratch_shapes=[pltpu.VMEM((8, d), F32)],
        compiler_params=pltpu.CompilerParams(
            dimension_semantics=("parallel", "arbitrary"), vmem_limit_bytes=VMEM_LIMIT),
        name="rwkv_prep",
    )(h3, *params)


def _scan_masks():
    c, n = CHUNK, GROUP_LANES
    row_s = lax.broadcasted_iota(jnp.int32, (c, n), 0)
    col_s = lax.broadcasted_iota(jnp.int32, (c, n), 1) % c
    strict = row_s > col_s
    incl = row_s >= col_s
    eye = row_s == col_s
    base = strict & ((row_s // BASE_BLOCK) == (col_s // BASE_BLOCK))
    offs = []
    b = BASE_BLOCK
    while b < c:
        offs.append(((row_s // (2 * b)) == (col_s // (2 * b)))
                    & ((row_s // b) % 2 == 1) & ((col_s // b) % 2 == 0))
        b *= 2
    row_b = lax.broadcasted_iota(jnp.int32, (n, n), 0)
    col_b = lax.broadcasted_iota(jnp.int32, (n, n), 1)
    mask_bd = (row_b // HEAD) == (col_b // HEAD)
    row_ge = {}
    shift = 1
    while shift < c:
        row_ge[shift] = row_s >= shift
        shift *= 2
    return mask_bd, strict, incl, eye, base, tuple(offs), row_ge


def _block_diag(z, mask_bd):
    tiled = jnp.concatenate([z] * HEADS_PER_GROUP, axis=0)
    return jnp.where(mask_bd, tiled, 0.0)


def _head_transpose(x):
    xt = x.T
    return jnp.concatenate([xt[h * HEAD:(h + 1) * HEAD, :] for h in range(HEADS_PER_GROUP)], axis=1)


def _scan_groups(r, k, v, lw, kk, a, s_cat, masks):
    mask_bd, strict, incl, eye, base, offs, row_ge = masks
    c, n = CHUNK, GROUP_LANES
    groups = range(len(r))
    bd = lambda z: _block_diag(z, mask_bd)
    cat0 = lambda *xs: jnp.concatenate(xs, axis=0)

    l_cum = list(lw)
    shift = 1
    while shift < c:
        l_cum = [l_cum[g] + jnp.where(row_ge[shift], pltpu.roll(l_cum[g], shift=shift, axis=0), 0.0)
                 for g in groups]
        shift *= 2
    e_l = [jnp.exp(l_cum[g]) for g in groups]
    e_nl = [jnp.exp(-l_cum[g]) for g in groups]
    a_t = [-kk[g] * jnp.exp(l_cum[g] - lw[g]) for g in groups]
    r_t = [r[g] * e_l[g] for g in groups]
    b_t = [kk[g] * a[g] * e_nl[g] for g in groups]
    k_t = [k[g] * e_nl[g] for g in groups]
    p_end = [e_l[g][c - 1:c, :] for g in groups]
    b_ht = [_head_transpose(b_t[g] * p_end[g]) for g in groups]
    k_ht = [_head_transpose(k_t[g] * p_end[g]) for g in groups]

    a_all = [_mm_nt(cat0(a_t[g], r_t[g]), cat0(bd(b_t[g]), bd(k_t[g]))) for g in groups]
    a_ab = [a_all[g][:c, :n] for g in groups]
    xs = [_mm(cat0(a_t[g], r_t[g], jnp.where(eye, p_end[g], 0.0)), bd(s_cat[g])) for g in groups]
    av = [_mm(cat0(jnp.where(strict, a_all[g][:c, n:], 0.0), jnp.where(incl, a_all[g][c:, n:], 0.0),
                   k_ht[g]), bd(v[g])) for g in groups]

    p = [jnp.where(base, a_ab[g], 0.0) for g in groups]
    inv = [jnp.where(eye, 1.0, 0.0) + p[g] for g in groups]
    p = [_mm(p[g], bd(p[g])) for g in groups]
    for _ in range(BASE_BLOCK.bit_length() - 3):
        both = [_mm(cat0(p[g], inv[g]), bd(p[g])) for g in groups]
        p = [both[g][:c] for g in groups]
        inv = [inv[g] + both[g][c:] for g in groups]
    inv = [inv[g] + _mm(inv[g], bd(p[g])) for g in groups]
    for off in offs:
        x = [_mm(jnp.where(off, a_ab[g], 0.0), bd(inv[g])) for g in groups]
        inv = [inv[g] + _mm(inv[g], bd(x[g])) for g in groups]

    u = [_mm(inv[g], bd(xs[g][:c] + av[g][:c])) for g in groups]
    yu = [_mm(cat0(jnp.where(incl, a_all[g][c:, :n], 0.0), b_ht[g]), bd(u[g])) for g in groups]
    y = [xs[g][c:2 * c] + yu[g][:c] + av[g][c:2 * c] for g in groups]
    s_new = [xs[g][2 * c:] + yu[g][c:] + av[g][2 * c:] for g in groups]
    return y, s_new


def _rwkv_scan_kernel(r_ref, k_ref, v_ref, lw_ref, kk_ref, a_ref, y_ref, s_ref):
    @pl.when(pl.program_id(1) == 0)
    def _():
        s_ref[...] = jnp.zeros_like(s_ref)

    nb, _, d = r_ref.shape
    where = [(bi, slice(lo, lo + GROUP_LANES)) for bi in range(nb) for lo in range(0, d, GROUP_LANES)]
    load = lambda ref: [ref[bi, :, sl] for bi, sl in where]
    y, s_new = _scan_groups(load(r_ref), load(k_ref), load(v_ref), load(lw_ref), load(kk_ref),
                            load(a_ref), [s_ref[g] for g in range(len(where))], _scan_masks())
    for g, (bi, sl) in enumerate(where):
        y_ref[bi, :, sl] = y[g]
        s_ref[g] = s_new[g]


def _rwkv_scan(r, k, v, lw, kk, a, *, batch_per_step=4):
    b, t, d = r.shape
    nb = batch_per_step
    tile = pl.BlockSpec((nb, CHUNK, d), lambda i, j: (i, j, 0))
    return pl.pallas_call(
        _rwkv_scan_kernel,
        out_shape=jax.ShapeDtypeStruct((b, t, d), F32),
        grid=(b // nb, t // CHUNK),
        in_specs=[tile] * 6,
        out_specs=tile,
        scratch_shapes=[pltpu.VMEM((nb * d // GROUP_LANES, HEAD, GROUP_LANES), F32)],
        compiler_params=pltpu.CompilerParams(
            dimension_semantics=("parallel", "arbitrary"), vmem_limit_bytes=VMEM_LIMIT),
        name="rwkv_scan",
    )(r, k, v, lw, kk, a)


def _attn_kernel(sinks_ref, h_ref, kvtp_ref, kvtc_ref, gpre_ref, wqt_ref, bqt_ref,
                 wo_ref, bo_ref, gpost_ref, o_ref):
    step = pl.program_id(1)
    w = WINDOW
    h = h_ref[0]
    u = _rms(h, gpre_ref[...])
    qt = ((_mm_nt(wqt_ref[...], u) + bqt_ref[...]) * (LOG2E * HEAD ** -0.5)).astype(BF16)
    kvt_cur, kvt_prev = kvtc_ref[0], kvtp_ref[0]
    kvw = kvt_cur.shape[0] // 2
    lanes = GROUP * w

    def band(n):
        return (jnp.concatenate([kvt_prev, kvt_cur[:, :w]], axis=1) if n == 0
                else kvt_cur[:, (n - 1) * w:(n + 1) * w])

    si = lax.broadcasted_iota(jnp.int32, (2 * w, lanes), 0)
    qi = lax.broadcasted_iota(jnp.int32, (2 * w, lanes), 1) % w
    valid = (si > qi) & (si <= qi + w)
    valid_first = valid & ((si >= w) | (step > 0))
    seg = lax.broadcasted_iota(jnp.int32, (1, lanes), 1) // w

    def scores(n):
        kvt_band = band(n)
        out = []
        for kh in range(KV_HEADS):
            q_stack = jnp.concatenate(
                [qt[(kh * GROUP + g) * HEAD:(kh * GROUP + g + 1) * HEAD, n * w:(n + 1) * w]
                 for g in range(GROUP)], axis=1)
            out.append(lax.dot_general(kvt_band[kh * HEAD:(kh + 1) * HEAD], q_stack,
                                       (((0,), (0,)), ((), ())), preferred_element_type=F32))
        return out

    def softmax(n, st):
        mask = valid_first if n == 0 else valid
        out = []
        for kh in range(KV_HEADS):
            sink = sinks_ref[kh * GROUP] * LOG2E
            for g in range(1, GROUP):
                sink = jnp.where(seg == g, sinks_ref[kh * GROUP + g] * LOG2E, sink)
            s = jnp.where(mask, st[kh], MASK_VALUE)
            mx = jnp.maximum(jnp.max(s, axis=0, keepdims=True), sink)
            p = jnp.exp2(s - mx)
            denom = jnp.sum(p, axis=0, keepdims=True) + jnp.exp2(sink - mx)
            out.append((p.astype(BF16), 1.0 / denom))
        return out

    def values(n, pd):
        vt_band = band(n)[kvw:]
        heads = []
        for kh in range(KV_HEADS):
            p, inv_denom = pd[kh]
            ot = jnp.dot(vt_band[kh * HEAD:(kh + 1) * HEAD], p, preferred_element_type=F32) * inv_denom
            heads += [ot[:, g * w:(g + 1) * w] for g in range(GROUP)]
        return jnp.concatenate(heads, axis=0).astype(BF16)

    st = [scores(0)]
    cols = []
    for n in range(ATTN_BLOCKS):
        if n + 1 < ATTN_BLOCKS:
            st.append(scores(n + 1))
        cols.append(values(n, softmax(n, st[n])))
    ot_all = jnp.concatenate(cols, axis=1)
    m = lax.dot_general(ot_all, wo_ref[...], (((0,), (0,)), ((), ())),
                        preferred_element_type=F32) + bo_ref[...]
    o_ref[0] = h + _rms(m, gpost_ref[...])


def _attn_block(h3, kvt3, sinks, g_pre, w_qt, b_qt, w_o, b_o, g_post):
    b, t, d = h3.shape
    kv2 = kvt3.shape[1]
    tq = ATTN_BLOCKS * WINDOW
    tile = pl.BlockSpec((1, tq, d), lambda i, j: (i, j, 0))
    prev = lambda j: jnp.maximum(j * ATTN_BLOCKS - 1, 0)
    params = (g_pre, w_qt, b_qt, w_o, b_o, g_post)
    return pl.pallas_call(
        _attn_kernel,
        out_shape=jax.ShapeDtypeStruct((b, t, d), F32),
        grid=(b, t // tq),
        in_specs=[
            pl.BlockSpec(memory_space=pltpu.SMEM),
            tile,
            pl.BlockSpec((1, kv2, WINDOW), lambda i, j: (i, 0, prev(j))),
            pl.BlockSpec((1, kv2, tq), lambda i, j: (i, 0, j)),
        ] + [_full_spec(p) for p in params],
        out_specs=tile,
        compiler_params=pltpu.CompilerParams(
            dimension_semantics=("parallel", "parallel"), vmem_limit_bytes=VMEM_LIMIT),
        name="swa_block",
    )(sinks, h3, kvt3, kvt3, *params)


def kernel(x, norm_g, ffn_w_in, ffn_w_out, rwkv_mu, rwkv_w_rkv, rwkv_w_o, rwkv_w0, rwkv_w1, rwkv_w2, rwkv_a0, rwkv_a1, rwkv_a2, rwkv_g1, rwkv_g2, rwkv_k_k, rwkv_k_a, rwkv_r_k, rwkv_gn_g, rwkv_gn_b, kv_norm_g, w_kv, b_kv, attn_w_q, attn_b_q, attn_w_o, attn_b_o, attn_sinks):
    b, t, d = x.shape
    m = b * t
    depth = norm_g.shape[0]
    n_a = rwkv_mu.shape[0]
    row = lambda vec: vec.reshape(1, -1).astype(F32)
    col = lambda vec: vec.reshape(-1, 1).astype(F32)
    bf = lambda w: w.astype(BF16)

    lane_head = jnp.arange(GROUP_LANES) // HEAD
    ones_bd = (lane_head[:, None] == lane_head[None, :]).astype(BF16)

    h = x.reshape(m, d)
    kvt_sh = None
    w_in, w_out = ffn_w_in.astype(F32), ffn_w_out.astype(F32)
    kv_params = (row(kv_norm_g), bf(w_kv.T), col(b_kv))
    for layer in range(depth):
        g = norm_g[layer]
        h = _ffn_block(h, row(g[0]), w_in, w_out, row(g[1]), layer, 0)
        mix = None
        if layer < n_a:
            i = layer
            params = (row(g[2]), rwkv_mu[i], bf(rwkv_w_rkv[i]),
                      row(rwkv_w0[i]), bf(rwkv_w1[i]), bf(rwkv_w2[i]),
                      row(rwkv_a0[i]), bf(rwkv_a1[i]), bf(rwkv_a2[i]), bf(rwkv_g1[i]), bf(rwkv_g2[i]),
                      row(rwkv_k_k[i]), row(rwkv_k_a[i]), row(rwkv_r_k[i]), ones_bd)
            r, k, v, lw, kk, a, gate, bonus = _rwkv_prep(h.reshape(b, t, d), params)
            y = _rwkv_scan(r, k, v, lw, kk, a)
            mix = (y.reshape(m, d), bonus.reshape(m, d), gate.reshape(m, d), row(rwkv_gn_g[i]),
                   row(rwkv_gn_b[i]), bf(rwkv_w_o[i]), row(g[3]), ones_bd)
        else:
            j = layer - n_a
            h = _attn_block(h.reshape(b, t, d), kvt_sh, attn_sinks[j].astype(F32), row(g[2]),
                            bf(attn_w_q[j].T), col(attn_b_q[j]), bf(attn_w_o[j]), row(attn_b_o[j]),
                            row(g[3])).reshape(m, d)
        if layer == n_a - 1:
            h, kvt_sh = _ffn_block(h, row(g[4]), w_in, w_out, row(g[5]), layer, 1, mix=mix,
                                   kv=kv_params, seq_len=t)
        else:
            h = _ffn_block(h, row(g[4]), w_in, w_out, row(g[5]), layer, 1, mix=mix)
    return h.reshape(b, t, d)
```

```python
import functools

import jax
import jax.numpy as jnp
from jax import lax
from jax.experimental import pallas as pl
from jax.experimental.pallas import tpu as pltpu

F32 = jnp.float32
BF16 = jnp.bfloat16

RMS_EPS = 1e-6
GN_EPS = 64e-5
HEAD = 64
WINDOW = 128
MASK_VALUE = -1e30
KV_HEADS = 4
GROUP = 4
ATTN_BLOCKS = 4
LOG2E = 1.4426950408889634

CHUNK = 64
GROUP_LANES = 256
HEADS_PER_GROUP = GROUP_LANES // HEAD
BASE_BLOCK = 8
PREP_ROWS = 512
FFN_ROWS = 1024
FFN_TILE = 1024
FFN_MIX_TILE = 512
EXP_MINUS_HALF = 0.6065306597126334

VMEM_LIMIT = 56 * 1024 * 1024
MXU_WIDTH = 256


def _rms(x, g):
    return x * lax.rsqrt(jnp.mean(x * x, axis=-1, keepdims=True) + RMS_EPS) * g


def _mm(a, b):
    return jnp.dot(a.astype(BF16), b.astype(BF16), preferred_element_type=F32)


def _mm_nt(a, b):
    return lax.dot_general(a.astype(BF16), b.astype(BF16), (((1,), (1,)), ((), ())),
                           preferred_element_type=F32)


def _split2(x):
    hi = x.astype(BF16)
    lo = (x - hi.astype(F32)).astype(BF16)
    return hi, lo


def _seg_sum(x, ones_bd, pieces=1):
    m, d = x.shape
    xs = jnp.concatenate([x[:, lo:lo + GROUP_LANES] for lo in range(0, d, GROUP_LANES)], axis=0)
    parts = (xs.astype(BF16),) if pieces == 1 else _split2(xs)
    s = sum(jnp.dot(p, ones_bd, preferred_element_type=F32) for p in parts)
    return jnp.concatenate([s[i * m:(i + 1) * m] for i in range(d // GROUP_LANES)], axis=1)


def _full_spec(x):
    return pl.BlockSpec(x.shape, lambda *_: (0,) * x.ndim)


def _ffn_chunks(d_ff):
    tiles = d_ff // MXU_WIDTH
    first = (tiles + 1) // 2 * MXU_WIDTH
    return ((0, first), (first, d_ff))


def _weight_steps(d_ff):
    return d_ff // MXU_WIDTH


def _ffn_kernel(*refs, has_mix, has_kv):
    refs = list(refs)
    h_ref = refs.pop(0)
    mix_refs = [refs.pop(0) for _ in range(8)] if has_mix else None
    gpre_ref, win_chunk_ref, wo_chunk_ref, gpost_ref = [refs.pop(0) for _ in range(4)]
    kv_refs = [refs.pop(0) for _ in range(3)] if has_kv else None
    o_ref = refs.pop(0)
    kv_out_refs = [refs.pop(0)] if has_kv else []
    win_ref, wo_ref = refs[:2]
    hmix_ref = refs[2] if has_mix else None
    d_ff = wo_ref.shape[0]
    step = pl.program_id(0)
    nw = _weight_steps(d_ff)

    cw, cr = win_chunk_ref.shape[1], wo_chunk_ref.shape[0]
    for c in range(nw):
        @pl.when(step == c)
        def _(c=c):
            win_ref[:, c * cw:(c + 1) * cw] = win_chunk_ref[...].astype(BF16)
            wo_ref[c * cr:(c + 1) * cr, :] = wo_chunk_ref[...].astype(BF16)

    if has_mix:
        @pl.when(step == 0)
        def _():
            stages = _mix_stages(h_ref, mix_refs)
            for stage in stages[:-1]:
                stage()
            hmix_ref[0] = stages[-1]()

    @pl.when(step >= nw)
    def _():
        if has_mix:
            slot = (step - nw) % 2
            stages = _mix_stages(h_ref, mix_refs)
            h_next = _ffn_rows(hmix_ref[slot], gpre_ref, win_ref, wo_ref, gpost_ref, kv_refs, o_ref,
                               kv_out_refs, d_ff, slice(None), stages)
            hmix_ref[1 - slot] = h_next
        else:
            sub = min(FFN_ROWS, h_ref.shape[0])
            for r0 in range(0, h_ref.shape[0], sub):
                rows = slice(r0, r0 + sub)
                _ffn_rows(h_ref[rows, :], gpre_ref, win_ref, wo_ref, gpost_ref, kv_refs, o_ref,
                          kv_out_refs, d_ff, rows, None)


def _mix_stages(h_ref, mix_refs):
    y_ref, bonus_ref, gate_ref, gng_ref, gnb_ref, wmix_ref, gmix_ref, ones_ref = mix_refs
    val = {}

    def mean():
        val["dev"] = y_ref[...] - _seg_sum(y_ref[...], ones_ref[...], pieces=2) * (1.0 / HEAD)

    def variance():
        val["var"] = _seg_sum(val["dev"] * val["dev"], ones_ref[...]) * (1.0 / HEAD)

    def project():
        yn = val["dev"] * lax.rsqrt(val["var"] + GN_EPS) * gng_ref[...] + gnb_ref[...] + bonus_ref[...]
        val["m"] = _mm(yn * gate_ref[...], wmix_ref[...])

    def residual():
        return h_ref[...] + _rms(val["m"], gmix_ref[...])

    return [mean, variance, project, residual]


def _ffn_rows(h, gpre_ref, win_ref, wo_ref, gpost_ref, kv_refs, o_ref, kv_out_refs, d_ff, rows, side):
    side = list(side) if side else []
    run_side = lambda: side.pop(0)() if len(side) > 1 else None
    run_side()
    xn = _rms(h, gpre_ref[...]).astype(BF16)
    acc = None
    for lo, hi in _ffn_chunks(d_ff):
        gate = jnp.dot(xn, win_ref[:, lo:hi], preferred_element_type=F32)
        up = jnp.dot(xn, win_ref[:, d_ff + lo:d_ff + hi], preferred_element_type=F32)
        run_side()
        act = (gate * jax.nn.sigmoid(gate) * up).astype(BF16)
        part = jnp.dot(act, wo_ref[lo:hi, :], preferred_element_type=F32)
        acc = part if acc is None else acc + part
    while len(side) > 1:
        run_side()
    h = h + 0.5 * _rms(acc, gpost_ref[...])
    o_ref[rows, :] = h
    if kv_refs is not None:
        gkv_ref, wkvt_ref, bkvt_ref = kv_refs
        (kvt_ref,) = kv_out_refs
        u = _rms(h, gkv_ref[...]).astype(BF16)
        kvt_ref[0, :, rows] = (_mm_nt(wkvt_ref[...], u) + bkvt_ref[...]).astype(BF16)
    return side[0]() if side else None


def _ffn_block(h, g_pre, w_in, w_out, g_post, layer, which, *, mix=None, kv=None, seq_len=None):
    m, d = h.shape
    d_ff = w_out.shape[2]
    nw = _weight_steps(d_ff)
    tm = FFN_MIX_TILE if mix is not None else FFN_TILE
    row_tile = lambda i: jnp.maximum(i - nw, 0)
    chunk = lambda i: jnp.minimum(i, nw - 1)
    tile = pl.BlockSpec((tm, d), lambda i: (row_tile(i), 0))
    resident = lambda x: pl.BlockSpec(x.shape, lambda i: (0,) * x.ndim, pipeline_mode=pl.Buffered(1))
    scratch = [pltpu.VMEM((d, 2 * d_ff), BF16), pltpu.VMEM((d_ff, d), BF16)]
    if mix is None:
        args, in_specs = [h], [tile]
    else:
        ahead = pl.BlockSpec((tm, d), lambda i: (jnp.clip(i - nw + 1, 0, m // tm - 1), 0))
        args = [h] + list(mix)
        in_specs = [ahead] * 4 + [resident(p) for p in mix[3:]]
        scratch.append(pltpu.VMEM((2, tm, d), F32))
    args += [g_pre, w_in, w_out, g_post]
    in_specs += [resident(g_pre),
                 pl.BlockSpec((None, None, d, 2 * d_ff // nw), lambda i: (layer, which, 0, chunk(i))),
                 pl.BlockSpec((None, None, d_ff // nw, d), lambda i: (layer, which, chunk(i), 0)),
                 resident(g_post)]
    out_shape, out_specs = [jax.ShapeDtypeStruct((m, d), F32)], [tile]
    if kv is not None:
        args += list(kv)
        in_specs += [resident(p) for p in kv]
        kv2 = kv[1].shape[0]
        steps = seq_len // tm
        out_shape += [jax.ShapeDtypeStruct((m // seq_len, kv2, seq_len), BF16)]
        out_specs += [pl.BlockSpec((1, kv2, tm), lambda i: (row_tile(i) // steps, 0, row_tile(i) % steps))]
    out = pl.pallas_call(
        functools.partial(_ffn_kernel, has_mix=mix is not None, has_kv=kv is not None),
        out_shape=out_shape,
        grid=(nw + m // tm,),
        in_specs=in_specs,
        out_specs=out_specs,
        scratch_shapes=scratch,
        compiler_params=pltpu.CompilerParams(
            dimension_semantics=("arbitrary",), vmem_limit_bytes=VMEM_LIMIT),
        name="ffn_block",
    )(*args)
    return out if kv is not None else out[0]


def _rwkv_prep_kernel(h_ref, g_ref, mu_ref, wrkv_ref, w0_ref, w1_ref, w2_ref,
                      a0_ref, a1_ref, a2_ref, g1_ref, g2_ref, kk_ref, ka_ref, rk_ref,
                      ones_ref,
                      r_out, k_out, v_out, lw_out, kk_out, a_out, g_out, bonus_out,
                      carry_ref):
    @pl.when(pl.program_id(1) == 0)
    def _():
        carry_ref[...] = jnp.zeros_like(carry_ref)

    ones_bd = ones_ref[...]
    mu = mu_ref[...]
    tm = h_ref.shape[1]
    row = lax.broadcasted_iota(jnp.int32, (PREP_ROWS, h_ref.shape[2]), 0)
    last = carry_ref[0:1, :]
    for r0 in range(0, tm, PREP_ROWS):
        rows = slice(r0, r0 + PREP_ROWS)
        u = _rms(h_ref[0, rows, :], g_ref[...])
        prev = jnp.where(row == 0, last, pltpu.roll(u, shift=1, axis=0))
        last = u[PREP_ROWS - 1:PREP_ROWS, :]
        xx = prev - u
        xr, xw, xk, xv, xa, xg = [u + xx * mu[i:i + 1, :] for i in range(6)]

        w_mid = jnp.tanh(_mm(xw, w1_ref[...]))
        a_mid = _mm(xa, a1_ref[...])
        g_mid = jax.nn.sigmoid(_mm(xg, g1_ref[...]))
        r = _mm(xr, wrkv_ref[0])
        k = _mm(xk, wrkv_ref[1])
        v = _mm(xv, wrkv_ref[2])
        lw = jax.nn.sigmoid(w0_ref[...] + _mm(w_mid, w2_ref[...])) * (-EXP_MINUS_HALF)
        a = jax.nn.sigmoid(a0_ref[...] + _mm(a_mid, a2_ref[...]))
        g = _mm(g_mid, g2_ref[...])

        kk = k * kk_ref[...]
        norm = jnp.sqrt(_seg_sum(kk * kk, ones_bd))
        kk = kk / jnp.maximum(norm, 1e-12)
        k = k * (1.0 + (a - 1.0) * ka_ref[...])

        r_out[0, rows, :] = r
        k_out[0, rows, :] = k
        v_out[0, rows, :] = v.astype(BF16)
        lw_out[0, rows, :] = lw
        kk_out[0, rows, :] = kk
        a_out[0, rows, :] = a
        g_out[0, rows, :] = g.astype(BF16)
        bonus_out[0, rows, :] = (_seg_sum(r * k * rk_ref[...], ones_bd) * v).astype(BF16)
    carry_ref[0:1, :] = last


def _rwkv_prep(h3, params, *, tm=512):
    b, t, d = h3.shape
    tile = pl.BlockSpec((1, tm, d), lambda i, j: (i, j, 0))
    resident = lambda x: pl.BlockSpec(x.shape, lambda i, j: (0,) * x.ndim, pipeline_mode=pl.Buffered(1))
    dtypes = [F32, F32, BF16, F32, F32, F32, BF16, BF16]
    return pl.pallas_call(
        _rwkv_prep_kernel,
        out_shape=[jax.ShapeDtypeStruct((b, t, d), dt) for dt in dtypes],
        grid=(b, t // tm),
        in_specs=[tile] + [resident(p) for p in params],
        out_specs=[tile] * 8,
        scratch_shapes=[pltpu.VMEM((8, d), F32)],
        compiler_params=pltpu.CompilerParams(
            dimension_semantics=("parallel", "arbitrary"), vmem_limit_bytes=VMEM_LIMIT),
        name="rwkv_prep",
    )(h3, *params)


def _scan_masks():
    c, n = CHUNK, GROUP_LANES
    row_s = lax.broadcasted_iota(jnp.int32, (c, n), 0)
    col_s = lax.broadcasted_iota(jnp.int32, (c, n), 1) % c
    strict = row_s > col_s
    incl = row_s >= col_s
    eye = row_s == col_s
    base = strict & ((row_s // BASE_BLOCK) == (col_s // BASE_BLOCK))
    offs = []
    b = BASE_BLOCK
    while b < c:
        offs.append(((row_s // (2 * b)) == (col_s // (2 * b)))
                    & ((row_s // b) % 2 == 1) & ((col_s // b) % 2 == 0))
        b *= 2
    row_b = lax.broadcasted_iota(jnp.int32, (n, n), 0)
    col_b = lax.broadcasted_iota(jnp.int32, (n, n), 1)
    mask_bd = (row_b // HEAD) == (col_b // HEAD)
    row_ge = {}
    shift = 1
    while shift < c:
        row_ge[shift] = row_s >= shift
        shift *= 2
    return mask_bd, strict, incl, eye, base, tuple(offs), row_ge


def _block_diag(z, mask_bd):
    tiled = jnp.concatenate([z] * HEADS_PER_GROUP, axis=0)
    return jnp.where(mask_bd, tiled, 0.0)


def _head_transpose(x):
    xt = x.T
    return jnp.concatenate([xt[h * HEAD:(h + 1) * HEAD, :] for h in range(HEADS_PER_GROUP)], axis=1)


def _scan_groups(r, k, v, lw, kk, a, s_cat, masks):
    mask_bd, strict, incl, eye, base, offs, row_ge = masks
    c, n = CHUNK, GROUP_LANES
    groups = range(len(r))
    bd = lambda z: _block_diag(z, mask_bd)
    cat0 = lambda *xs: jnp.concatenate(xs, axis=0)

    l_cum = list(lw)
    shift = 1
    while shift < c:
        l_cum = [l_cum[g] + jnp.where(row_ge[shift], pltpu.roll(l_cum[g], shift=shift, axis=0), 0.0)
                 for g in groups]
        shift *= 2
    e_l = [jnp.exp(l_cum[g]) for g in groups]
    e_nl = [jnp.exp(-l_cum[g]) for g in groups]
    a_t = [-kk[g] * jnp.exp(l_cum[g] - lw[g]) for g in groups]
    r_t = [r[g] * e_l[g] for g in groups]
    b_t = [kk[g] * a[g] * e_nl[g] for g in groups]
    k_t = [k[g] * e_nl[g] for g in groups]
    p_end = [e_l[g][c - 1:c, :] for g in groups]
    b_ht = [_head_transpose(b_t[g] * p_end[g]) for g in groups]
    k_ht = [_head_transpose(k_t[g] * p_end[g]) for g in groups]

    a_all = [_mm_nt(cat0(a_t[g], r_t[g]), cat0(bd(b_t[g]), bd(k_t[g]))) for g in groups]
    a_ab = [a_all[g][:c, :n] for g in groups]
    xs = [_mm(cat0(a_t[g], r_t[g], jnp.where(eye, p_end[g], 0.0)), bd(s_cat[g])) for g in groups]
    av = [_mm(cat0(jnp.where(strict, a_all[g][:c, n:], 0.0), jnp.where(incl, a_all[g][c:, n:], 0.0),
                   k_ht[g]), bd(v[g])) for g in groups]

    p = [jnp.where(base, a_ab[g], 0.0) for g in groups]
    inv = [jnp.where(eye, 1.0, 0.0) + p[g] for g in groups]
    p = [_mm(p[g], bd(p[g])) for g in groups]
    for _ in range(BASE_BLOCK.bit_length() - 3):
        both = [_mm(cat0(p[g], inv[g]), bd(p[g])) for g in groups]
        p = [both[g][:c] for g in groups]
        inv = [inv[g] + both[g][c:] for g in groups]
    inv = [inv[g] + _mm(inv[g], bd(p[g])) for g in groups]
    for off in offs:
        x = [_mm(jnp.where(off, a_ab[g], 0.0), bd(inv[g])) for g in groups]
        inv = [inv[g] + _mm(inv[g], bd(x[g])) for g in groups]

    u = [_mm(inv[g], bd(xs[g][:c] + av[g][:c])) for g in groups]
    yu = [_mm(cat0(jnp.where(incl, a_all[g][c:, :n], 0.0), b_ht[g]), bd(u[g])) for g in groups]
    y = [xs[g][c:2 * c] + yu[g][:c] + av[g][c:2 * c] for g in groups]
    s_new = [xs[g][2 * c:] + yu[g][c:] + av[g][2 * c:] for g in groups]
    return y, s_new


def _rwkv_scan_kernel(r_ref, k_ref, v_ref, lw_ref, kk_ref, a_ref, y_ref, s_ref):
    @pl.when(pl.program_id(1) == 0)
    def _():
        s_ref[...] = jnp.zeros_like(s_ref)

    nb, _, d = r_ref.shape
    where = [(bi, slice(lo, lo + GROUP_LANES)) for bi in range(nb) for lo in range(0, d, GROUP_LANES)]
    load = lambda ref: [ref[bi, :, sl] for bi, sl in where]
    y, s_new = _scan_groups(load(r_ref), load(k_ref), load(v_ref), load(lw_ref), load(kk_ref),
                            load(a_ref), [s_ref[g] for g in range(len(where))], _scan_masks())
    for g, (bi, sl) in enumerate(where):
        y_ref[bi, :, sl] = y[g]
        s_ref[g] = s_new[g]


def _rwkv_scan(r, k, v, lw, kk, a, *, batch_per_step=4):
    b, t, d = r.shape
    nb = batch_per_step
    tile = pl.BlockSpec((nb, CHUNK, d), lambda i, j: (i, j, 0))
    return pl.pallas_call(
        _rwkv_scan_kernel,
        out_shape=jax.ShapeDtypeStruct((b, t, d), F32),
        grid=(b // nb, t // CHUNK),
        in_specs=[tile] * 6,
        out_specs=tile,
        scratch_shapes=[pltpu.VMEM((nb * d // GROUP_LANES, HEAD, GROUP_LANES), F32)],
        compiler_params=pltpu.CompilerParams(
            dimension_semantics=("parallel", "arbitrary"), vmem_limit_bytes=VMEM_LIMIT),
        name="rwkv_scan",
    )(r, k, v, lw, kk, a)


def _attn_kernel(sinks_ref, h_ref, kvtp_ref, kvtc_ref, gpre_ref, wqt_ref, bqt_ref,
                 wo_ref, bo_ref, gpost_ref, o_ref):
    step = pl.program_id(1)
    w = WINDOW
    h = h_ref[0]
    u = _rms(h, gpre_ref[...])
    qt = ((_mm_nt(wqt_ref[...], u) + bqt_ref[...]) * (LOG2E * HEAD ** -0.5)).astype(BF16)
    kvt_cur, kvt_prev = kvtc_ref[0], kvtp_ref[0]
    kvw = kvt_cur.shape[0] // 2
    lanes = GROUP * w

    def band(n):
        return (jnp.concatenate([kvt_prev, kvt_cur[:, :w]], axis=1) if n == 0
                else kvt_cur[:, (n - 1) * w:(n + 1) * w])

    si = lax.broadcasted_iota(jnp.int32, (2 * w, lanes), 0)
    qi = lax.broadcasted_iota(jnp.int32, (2 * w, lanes), 1) % w
    valid = (si > qi) & (si <= qi + w)
    valid_first = valid & ((si >= w) | (step > 0))
    seg = lax.broadcasted_iota(jnp.int32, (1, lanes), 1) // w

    def scores(n):
        kvt_band = band(n)
        out = []
        for kh in range(KV_HEADS):
            q_stack = jnp.concatenate(
                [qt[(kh * GROUP + g) * HEAD:(kh * GROUP + g + 1) * HEAD, n * w:(n + 1) * w]
                 for g in range(GROUP)], axis=1)
            out.append(lax.dot_general(kvt_band[kh * HEAD:(kh + 1) * HEAD], q_stack,
                                       (((0,), (0,)), ((), ())), preferred_element_type=F32))
        return out

    def softmax(n, st):
        mask = valid_first if n == 0 else valid
        out = []
        for kh in range(KV_HEADS):
            sink = sinks_ref[kh * GROUP] * LOG2E
            for g in range(1, GROUP):
                sink = jnp.where(seg == g, sinks_ref[kh * GROUP + g] * LOG2E, sink)
            s = jnp.where(mask, st[kh], MASK_VALUE)
            mx = jnp.maximum(jnp.max(s, axis=0, keepdims=True), sink)
            p = jnp.exp2(s - mx)
            denom = jnp.sum(p, axis=0, keepdims=True) + jnp.exp2(sink - mx)
            out.append((p.astype(BF16), 1.0 / denom))
        return out

    def values(n, pd):
        vt_band = band(n)[kvw:]
        heads = []
        for kh in range(KV_HEADS):
            p, inv_denom = pd[kh]
            ot = jnp.dot(vt_band[kh * HEAD:(kh + 1) * HEAD], p, preferred_element_type=F32) * inv_denom
            heads += [ot[:, g * w:(g + 1) * w] for g in range(GROUP)]
        return jnp.concatenate(heads, axis=0).astype(BF16)

    st = [scores(0)]
    cols = []
    for n in range(ATTN_BLOCKS):
        if n + 1 < ATTN_BLOCKS:
            st.append(scores(n + 1))
        cols.append(values(n, softmax(n, st[n])))
    ot_all = jnp.concatenate(cols, axis=1)
    m = lax.dot_general(ot_all, wo_ref[...], (((0,), (0,)), ((), ())),
                        preferred_element_type=F32) + bo_ref[...]
    o_ref[0] = h + _rms(m, gpost_ref[...])


def _attn_block(h3, kvt3, sinks, g_pre, w_qt, b_qt, w_o, b_o, g_post):
    b, t, d = h3.shape
    kv2 = kvt3.shape[1]
    tq = ATTN_BLOCKS * WINDOW
    tile = pl.BlockSpec((1, tq, d), lambda i, j: (i, j, 0))
    prev = lambda j: jnp.maximum(j * ATTN_BLOCKS - 1, 0)
    params = (g_pre, w_qt, b_qt, w_o, b_o, g_post)
    return pl.pallas_call(
        _attn_kernel,
        out_shape=jax.ShapeDtypeStruct((b, t, d), F32),
        grid=(b, t // tq),
        in_specs=[
            pl.BlockSpec(memory_space=pltpu.SMEM),
            tile,
            pl.BlockSpec((1, kv2, WINDOW), lambda i, j: (i, 0, prev(j))),
            pl.BlockSpec((1, kv2, tq), lambda i, j: (i, 0, j)),
        ] + [_full_spec(p) for p in params],
        out_specs=tile,
        compiler_params=pltpu.CompilerParams(
            dimension_semantics=("parallel", "parallel"), vmem_limit_bytes=VMEM_LIMIT),
        name="swa_block",
    )(sinks, h3, kvt3, kvt3, *params)


def kernel(x, norm_g, ffn_w_in, ffn_w_out, rwkv_mu, rwkv_w_rkv, rwkv_w_o, rwkv_w0, rwkv_w1, rwkv_w2, rwkv_a0, rwkv_a1, rwkv_a2, rwkv_g1, rwkv_g2, rwkv_k_k, rwkv_k_a, rwkv_r_k, rwkv_gn_g, rwkv_gn_b, kv_norm_g, w_kv, b_kv, attn_w_q, attn_b_q, attn_w_o, attn_b_o, attn_sinks):
    b, t, d = x.shape
    m = b * t
    depth = norm_g.shape[0]
    n_a = rwkv_mu.shape[0]
    row = lambda vec: vec.reshape(1, -1).astype(F32)
    col = lambda vec: vec.reshape(-1, 1).astype(F32)
    bf = lambda w: w.astype(BF16)

    lane_head = jnp.arange(GROUP_LANES) // HEAD
    ones_bd = (lane_head[:, None] == lane_head[None, :]).astype(BF16)

    h = x.reshape(m, d)
    kvt_sh = None
    w_in, w_out = ffn_w_in.astype(F32), ffn_w_out.astype(F32)
    kv_params = (row(kv_norm_g), bf(w_kv.T), col(b_kv))
    for layer in range(depth):
        g = norm_g[layer]
        h = _ffn_block(h, row(g[0]), w_in, w_out, row(g[1]), layer, 0)
        mix = None
        if layer < n_a:
            i = layer
            params = (row(g[2]), rwkv_mu[i], bf(rwkv_w_rkv[i]),
                      row(rwkv_w0[i]), bf(rwkv_w1[i]), bf(rwkv_w2[i]),
                      row(rwkv_a0[i]), bf(rwkv_a1[i]), bf(rwkv_a2[i]), bf(rwkv_g1[i]), bf(rwkv_g2[i]),
                      row(rwkv_k_k[i]), row(rwkv_k_a[i]), row(rwkv_r_k[i]), ones_bd)
            r, k, v, lw, kk, a, gate, bonus = _rwkv_prep(h.reshape(b, t, d), params)
            y = _rwkv_scan(r, k, v, lw, kk, a)
            mix = (y.reshape(m, d), bonus.reshape(m, d), gate.reshape(m, d), row(rwkv_gn_g[i]),
                   row(rwkv_gn_b[i]), bf(rwkv_w_o[i]), row(g[3]), ones_bd)
        else:
            j = layer - n_a
            h = _attn_block(h.reshape(b, t, d), kvt_sh, attn_sinks[j].astype(F32), row(g[2]),
                            bf(attn_w_q[j].T), col(attn_b_q[j]), bf(attn_w_o[j]), row(attn_b_o[j]),
                            row(g[3])).reshape(m, d)
        if layer == n_a - 1:
            h, kvt_sh = _ffn_block(h, row(g[4]), w_in, w_out, row(g[5]), layer, 1, mix=mix,
                                   kv=kv_params, seq_len=t)
        else:
            h = _ffn_block(h, row(g[4]), w_in, w_out, row(g[5]), layer, 1, mix=mix)
    return h.reshape(b, t, d)
```

```python
import functools

import jax
import jax.numpy as jnp
from jax import lax
from jax.experimental import pallas as pl
from jax.experimental.pallas import tpu as pltpu

F32 = jnp.float32
BF16 = jnp.bfloat16

RMS_EPS = 1e-6
GN_EPS = 64e-5
HEAD = 64
WINDOW = 128
MASK_VALUE = -1e30
KV_HEADS = 4
GROUP = 4
ATTN_BLOCKS = 8
LOG2E = 1.4426950408889634

CHUNK = 64
GROUP_LANES = 256
HEADS_PER_GROUP = GROUP_LANES // HEAD
BASE_BLOCK = 8
PREP_ROWS = 512
FFN_ROWS = 1024
FFN_TILE = 1024
FFN_MIX_TILE = 512
EXP_MINUS_HALF = 0.6065306597126334

VMEM_LIMIT = 56 * 1024 * 1024
MXU_WIDTH = 256


def _rms(x, g):
    return x * lax.rsqrt(jnp.mean(x * x, axis=-1, keepdims=True) + RMS_EPS) * g


def _mm(a, b):
    return jnp.dot(a.astype(BF16), b.astype(BF16), preferred_element_type=F32)


def _mm_nt(a, b):
    return lax.dot_general(a.astype(BF16), b.astype(BF16), (((1,), (1,)), ((), ())),
                           preferred_element_type=F32)


def _split2(x):
    hi = x.astype(BF16)
    lo = (x - hi.astype(F32)).astype(BF16)
    return hi, lo


def _seg_sum(x, ones_bd, pieces=1):
    m, d = x.shape
    xs = jnp.concatenate([x[:, lo:lo + GROUP_LANES] for lo in range(0, d, GROUP_LANES)], axis=0)
    parts = (xs.astype(BF16),) if pieces == 1 else _split2(xs)
    s = sum(jnp.dot(p, ones_bd, preferred_element_type=F32) for p in parts)
    return jnp.concatenate([s[i * m:(i + 1) * m] for i in range(d // GROUP_LANES)], axis=1)


def _full_spec(x):
    return pl.BlockSpec(x.shape, lambda *_: (0,) * x.ndim)


def _ffn_chunks(d_ff):
    tiles = d_ff // MXU_WIDTH
    first = (tiles + 1) // 2 * MXU_WIDTH
    return ((0, first), (first, d_ff))


def _weight_steps(d_ff):
    return d_ff // MXU_WIDTH


def _ffn_kernel(*refs, has_mix, has_kv):
    refs = list(refs)
    h_ref = refs.pop(0)
    mix_refs = [refs.pop(0) for _ in range(8)] if has_mix else None
    gpre_ref, win_chunk_ref, wo_chunk_ref, gpost_ref = [refs.pop(0) for _ in range(4)]
    kv_refs = [refs.pop(0) for _ in range(3)] if has_kv else None
    o_ref = refs.pop(0)
    kv_out_refs = [refs.pop(0)] if has_kv else []
    win_ref, wo_ref = refs[:2]
    hmix_ref = refs[2] if has_mix else None
    d_ff = wo_ref.shape[0]
    step = pl.program_id(0)
    nw = _weight_steps(d_ff)

    cw, cr = win_chunk_ref.shape[1], wo_chunk_ref.shape[0]
    for c in range(nw):
        @pl.when(step == c)
        def _(c=c):
            win_ref[:, c * cw:(c + 1) * cw] = win_chunk_ref[...].astype(BF16)
            wo_ref[c * cr:(c + 1) * cr, :] = wo_chunk_ref[...].astype(BF16)

    if has_mix:
        @pl.when(step == 0)
        def _():
            stages = _mix_stages(h_ref, mix_refs)
            for stage in stages[:-1]:
                stage()
            hmix_ref[0] = stages[-1]()

    @pl.when(step >= nw)
    def _():
        if has_mix:
            slot = (step - nw) % 2
            stages = _mix_stages(h_ref, mix_refs)
            h_next = _ffn_rows(hmix_ref[slot], gpre_ref, win_ref, wo_ref, gpost_ref, kv_refs, o_ref,
                               kv_out_refs, d_ff, slice(None), stages)
            hmix_ref[1 - slot] = h_next
        else:
            sub = min(FFN_ROWS, h_ref.shape[0])
            for r0 in range(0, h_ref.shape[0], sub):
                rows = slice(r0, r0 + sub)
                _ffn_rows(h_ref[rows, :], gpre_ref, win_ref, wo_ref, gpost_ref, kv_refs, o_ref,
                          kv_out_refs, d_ff, rows, None)


def _mix_stages(h_ref, mix_refs):
    y_ref, bonus_ref, gate_ref, gng_ref, gnb_ref, wmix_ref, gmix_ref, ones_ref = mix_refs
    val = {}

    def mean():
        val["dev"] = y_ref[...] - _seg_sum(y_ref[...], ones_ref[...], pieces=2) * (1.0 / HEAD)

    def variance():
        val["var"] = _seg_sum(val["dev"] * val["dev"], ones_ref[...]) * (1.0 / HEAD)

    def project():
        yn = val["dev"] * lax.rsqrt(val["var"] + GN_EPS) * gng_ref[...] + gnb_ref[...] + bonus_ref[...]
        val["m"] = _mm(yn * gate_ref[...], wmix_ref[...])

    def residual():
        return h_ref[...] + _rms(val["m"], gmix_ref[...])

    return [mean, variance, project, residual]


def _ffn_rows(h, gpre_ref, win_ref, wo_ref, gpost_ref, kv_refs, o_ref, kv_out_refs, d_ff, rows, side):
    side = list(side) if side else []
    run_side = lambda: side.pop(0)() if len(side) > 1 else None
    run_side()
    xn = _rms(h, gpre_ref[...]).astype(BF16)
    acc = None
    for lo, hi in _ffn_chunks(d_ff):
        gate = jnp.dot(xn, win_ref[:, lo:hi], preferred_element_type=F32)
        up = jnp.dot(xn, win_ref[:, d_ff + lo:d_ff + hi], preferred_element_type=F32)
        run_side()
        act = (gate * jax.nn.sigmoid(gate) * up).astype(BF16)
        part = jnp.dot(act, wo_ref[lo:hi, :], preferred_element_type=F32)
        acc = part if acc is None else acc + part
    while len(side) > 1:
        run_side()
    h = h + 0.5 * _rms(acc, gpost_ref[...])
    o_ref[rows, :] = h
    if kv_refs is not None:
        gkv_ref, wkvt_ref, bkvt_ref = kv_refs
        (kvt_ref,) = kv_out_refs
        u = _rms(h, gkv_ref[...]).astype(BF16)
        kvt_ref[0, :, rows] = (_mm_nt(wkvt_ref[...], u) + bkvt_ref[...]).astype(BF16)
    return side[0]() if side else None


def _ffn_block(h, g_pre, w_in, w_out, g_post, layer, which, *, mix=None, kv=None, seq_len=None):
    m, d = h.shape
    d_ff = w_out.shape[2]
    nw = _weight_steps(d_ff)
    tm = FFN_MIX_TILE if mix is not None else FFN_TILE
    row_tile = lambda i: jnp.maximum(i - nw, 0)
    chunk = lambda i: jnp.minimum(i, nw - 1)
    tile = pl.BlockSpec((tm, d), lambda i: (row_tile(i), 0))
    resident = lambda x: pl.BlockSpec(x.shape, lambda i: (0,) * x.ndim, pipeline_mode=pl.Buffered(1))
    scratch = [pltpu.VMEM((d, 2 * d_ff), BF16), pltpu.VMEM((d_ff, d), BF16)]
    if mix is None:
        args, in_specs = [h], [tile]
    else:
        ahead = pl.BlockSpec((tm, d), lambda i: (jnp.clip(i - nw + 1, 0, m // tm - 1), 0))
        args = [h] + list(mix)
        in_specs = [ahead] * 4 + [resident(p) for p in mix[3:]]
        scratch.append(pltpu.VMEM((2, tm, d), F32))
    args += [g_pre, w_in, w_out, g_post]
    in_specs += [resident(g_pre),
                 pl.BlockSpec((None, None, d, 2 * d_ff // nw), lambda i: (layer, which, 0, chunk(i))),
                 pl.BlockSpec((None, None, d_ff // nw, d), lambda i: (layer, which, chunk(i), 0)),
                 resident(g_post)]
    out_shape, out_specs = [jax.ShapeDtypeStruct((m, d), F32)], [tile]
    if kv is not None:
        args += list(kv)
        in_specs += [resident(p) for p in kv]
        kv2 = kv[1].shape[0]
        steps = seq_len // tm
        out_shape += [jax.ShapeDtypeStruct((m // seq_len, kv2, seq_len), BF16)]
        out_specs += [pl.BlockSpec((1, kv2, tm), lambda i: (row_tile(i) // steps, 0, row_tile(i) % steps))]
    out = pl.pallas_call(
        functools.partial(_ffn_kernel, has_mix=mix is not None, has_kv=kv is not None),
        out_shape=out_shape,
        grid=(nw + m // tm,),
        in_specs=in_specs,
        out_specs=out_specs,
        scratch_shapes=scratch,
        compiler_params=pltpu.CompilerParams(
            dimension_semantics=("arbitrary",), vmem_limit_bytes=VMEM_LIMIT),
        name="ffn_block",
    )(*args)
    return out if kv is not None else out[0]


def _rwkv_prep_kernel(h_ref, g_ref, mu_ref, wrkv_ref, w0_ref, w1_ref, w2_ref,
                      a0_ref, a1_ref, a2_ref, g1_ref, g2_ref, kk_ref, ka_ref, rk_ref,
                      ones_ref,
                      r_out, k_out, v_out, lw_out, kk_out, a_out, g_out, bonus_out,
                      carry_ref):
    @pl.when(pl.program_id(1) == 0)
    def _():
        carry_ref[...] = jnp.zeros_like(carry_ref)

    ones_bd = ones_ref[...]
    mu = mu_ref[...]
    tm = h_ref.shape[1]
    row = lax.broadcasted_iota(jnp.int32, (PREP_ROWS, h_ref.shape[2]), 0)
    last = carry_ref[0:1, :]
    for r0 in range(0, tm, PREP_ROWS):
        rows = slice(r0, r0 + PREP_ROWS)
        u = _rms(h_ref[0, rows, :], g_ref[...])
        prev = jnp.where(row == 0, last, pltpu.roll(u, shift=1, axis=0))
        last = u[PREP_ROWS - 1:PREP_ROWS, :]
        xx = prev - u
        xr, xw, xk, xv, xa, xg = [u + xx * mu[i:i + 1, :] for i in range(6)]

        w_mid = jnp.tanh(_mm(xw, w1_ref[...]))
        a_mid = _mm(xa, a1_ref[...])
        g_mid = jax.nn.sigmoid(_mm(xg, g1_ref[...]))
        r = _mm(xr, wrkv_ref[0])
        k = _mm(xk, wrkv_ref[1])
        v = _mm(xv, wrkv_ref[2])
        lw = jax.nn.sigmoid(w0_ref[...] + _mm(w_mid, w2_ref[...])) * (-EXP_MINUS_HALF)
        a = jax.nn.sigmoid(a0_ref[...] + _mm(a_mid, a2_ref[...]))
        g = _mm(g_mid, g2_ref[...])

        kk = k * kk_ref[...]
        norm = jnp.sqrt(_seg_sum(kk * kk, ones_bd))
        kk = kk / jnp.maximum(norm, 1e-12)
        k = k * (1.0 + (a - 1.0) * ka_ref[...])

        r_out[0, rows, :] = r
        k_out[0, rows, :] = k
        v_out[0, rows, :] = v.astype(BF16)
        lw_out[0, rows, :] = lw
        kk_out[0, rows, :] = kk
        a_out[0, rows, :] = a
        g_out[0, rows, :] = g.astype(BF16)
        bonus_out[0, rows, :] = (_seg_sum(r * k * rk_ref[...], ones_bd) * v).astype(BF16)
    carry_ref[0:1, :] = last


def _rwkv_prep(h3, params, *, tm=512):
    b, t, d = h3.shape
    tile = pl.BlockSpec((1, tm, d), lambda i, j: (i, j, 0))
    resident = lambda x: pl.BlockSpec(x.shape, lambda i, j: (0,) * x.ndim, pipeline_mode=pl.Buffered(1))
    dtypes = [F32, F32, BF16, F32, F32, F32, BF16, BF16]
    return pl.pallas_call(
        _rwkv_prep_kernel,
        out_shape=[jax.ShapeDtypeStruct((b, t, d), dt) for dt in dtypes],
        grid=(b, t // tm),
        in_specs=[tile] + [resident(p) for p in params],
        out_specs=[tile] * 8,
        scratch_shapes=[pltpu.VMEM((8, d), F32)],
        compiler_params=pltpu.CompilerParams(
            dimension_semantics=("parallel", "arbitrary"), vmem_limit_bytes=VMEM_LIMIT),
        name="rwkv_prep",
    )(h3, *params)


def _scan_masks():
    c, n = CHUNK, GROUP_LANES
    row_s = lax.broadcasted_iota(jnp.int32, (c, n), 0)
    col_s = lax.broadcasted_iota(jnp.int32, (c, n), 1) % c
    strict = row_s > col_s
    incl = row_s >= col_s
    eye = row_s == col_s
    base = strict & ((row_s // BASE_BLOCK) == (col_s // BASE_BLOCK))
    offs = []
    b = BASE_BLOCK
    while b < c:
        offs.append(((row_s // (2 * b)) == (col_s // (2 * b)))
                    & ((row_s // b) % 2 == 1) & ((col_s // b) % 2 == 0))
        b *= 2
    row_b = lax.broadcasted_iota(jnp.int32, (n, n), 0)
    col_b = lax.broadcasted_iota(jnp.int32, (n, n), 1)
    mask_bd = (row_b // HEAD) == (col_b // HEAD)
    row_ge = {}
    shift = 1
    while shift < c:
        row_ge[shift] = row_s >= shift
        shift *= 2
    return mask_bd, strict, incl, eye, base, tuple(offs), row_ge


def _block_diag(z, mask_bd):
    tiled = jnp.concatenate([z] * HEADS_PER_GROUP, axis=0)
    return jnp.where(mask_bd, tiled, 0.0)


def _head_transpose(x):
    xt = x.T
    return jnp.concatenate([xt[h * HEAD:(h + 1) * HEAD, :] for h in range(HEADS_PER_GROUP)], axis=1)


def _scan_groups(r, k, v, lw, kk, a, s_cat, masks):
    mask_bd, strict, incl, eye, base, offs, row_ge = masks
    c, n = CHUNK, GROUP_LANES
    groups = range(len(r))
    bd = lambda z: _block_diag(z, mask_bd)
    cat0 = lambda *xs: jnp.concatenate(xs, axis=0)

    l_cum = list(lw)
    shift = 1
    while shift < c:
        l_cum = [l_cum[g] + jnp.where(row_ge[shift], pltpu.roll(l_cum[g], shift=shift, axis=0), 0.0)
                 for g in groups]
        shift *= 2
    e_l = [jnp.exp(l_cum[g]) for g in groups]
    e_nl = [jnp.exp(-l_cum[g]) for g in groups]
    a_t = [-kk[g] * jnp.exp(l_cum[g] - lw[g]) for g in groups]
    r_t = [r[g] * e_l[g] for g in groups]
    b_t = [kk[g] * a[g] * e_nl[g] for g in groups]
    k_t = [k[g] * e_nl[g] for g in groups]
    p_end = [e_l[g][c - 1:c, :] for g in groups]
    b_ht = [_head_transpose(b_t[g] * p_end[g]) for g in groups]
    k_ht = [_head_transpose(k_t[g] * p_end[g]) for g in groups]

    a_all = [_mm_nt(cat0(a_t[g], r_t[g]), cat0(bd(b_t[g]), bd(k_t[g]))) for g in groups]
    a_ab = [a_all[g][:c, :n] for g in groups]
    xs = [_mm(cat0(a_t[g], r_t[g], jnp.where(eye, p_end[g], 0.0)), bd(s_cat[g])) for g in groups]
    av = [_mm(cat0(jnp.where(strict, a_all[g][:c, n:], 0.0), jnp.where(incl, a_all[g][c:, n:], 0.0),
                   k_ht[g]), bd(v[g])) for g in groups]

    p = [jnp.where(base, a_ab[g], 0.0) for g in groups]
    inv = [jnp.where(eye, 1.0, 0.0) + p[g] for g in groups]
    p = [_mm(p[g], bd(p[g])) for g in groups]
    for _ in range(BASE_BLOCK.bit_length() - 3):
        both = [_mm(cat0(p[g], inv[g]), bd(p[g])) for g in groups]
        p = [both[g][:c] for g in groups]
        inv = [inv[g] + both[g][c:] for g in groups]
    inv = [inv[g] + _mm(inv[g], bd(p[g])) for g in groups]
    for off in offs:
        x = [_mm(jnp.where(off, a_ab[g], 0.0), bd(inv[g])) for g in groups]
        inv = [inv[g] + _mm(inv[g], bd(x[g])) for g in groups]

    u = [_mm(inv[g], bd(xs[g][:c] + av[g][:c])) for g in groups]
    yu = [_mm(cat0(jnp.where(incl, a_all[g][c:, :n], 0.0), b_ht[g]), bd(u[g])) for g in groups]
    y = [xs[g][c:2 * c] + yu[g][:c] + av[g][c:2 * c] for g in groups]
    s_new = [xs[g][2 * c:] + yu[g][c:] + av[g][2 * c:] for g in groups]
    return y, s_new


def _rwkv_scan_kernel(r_ref, k_ref, v_ref, lw_ref, kk_ref, a_ref, y_ref, s_ref):
    @pl.when(pl.program_id(1) == 0)
    def _():
        s_ref[...] = jnp.zeros_like(s_ref)

    nb, rows, d = r_ref.shape
    where = [(bi, slice(lo, lo + GROUP_LANES)) for bi in range(nb) for lo in range(0, d, GROUP_LANES)]
    masks = _scan_masks()
    state = [s_ref[g] for g in range(len(where))]
    for c0 in range(0, rows, CHUNK):
        load = lambda ref: [ref[bi, c0:c0 + CHUNK, sl] for bi, sl in where]
        y, state = _scan_groups(load(r_ref), load(k_ref), load(v_ref), load(lw_ref), load(kk_ref),
                                load(a_ref), state, masks)
        for g, (bi, sl) in enumerate(where):
            y_ref[bi, c0:c0 + CHUNK, sl] = y[g]
    for g in range(len(where)):
        s_ref[g] = state[g]


def _rwkv_scan(r, k, v, lw, kk, a, *, batch_per_step=4, chunks_per_step=2):
    b, t, d = r.shape
    nb = batch_per_step
    rows = chunks_per_step * CHUNK
    tile = pl.BlockSpec((nb, rows, d), lambda i, j: (i, j, 0))
    return pl.pallas_call(
        _rwkv_scan_kernel,
        out_shape=jax.ShapeDtypeStruct((b, t, d), F32),
        grid=(b // nb, t // rows),
        in_specs=[tile] * 6,
        out_specs=tile,
        scratch_shapes=[pltpu.VMEM((nb * d // GROUP_LANES, HEAD, GROUP_LANES), F32)],
        compiler_params=pltpu.CompilerParams(
            dimension_semantics=("parallel", "arbitrary"), vmem_limit_bytes=VMEM_LIMIT),
        name="rwkv_scan",
    )(r, k, v, lw, kk, a)


def _attn_kernel(sinks_ref, h_ref, kvtp_ref, kvtc_ref, gpre_ref, wqt_ref, bqt_ref,
                 wo_ref, bo_ref, gpost_ref, o_ref):
    step = pl.program_id(1)
    w = WINDOW
    h = h_ref[0]
    u = _rms(h, gpre_ref[...])
    qt = ((_mm_nt(wqt_ref[...], u) + bqt_ref[...]) * (LOG2E * HEAD ** -0.5)).astype(BF16)
    kvt_cur, kvt_prev = kvtc_ref[0], kvtp_ref[0]
    kvw = kvt_cur.shape[0] // 2
    lanes = GROUP * w

    def band(n):
        return (jnp.concatenate([kvt_prev, kvt_cur[:, :w]], axis=1) if n == 0
                else kvt_cur[:, (n - 1) * w:(n + 1) * w])

    si = lax.broadcasted_iota(jnp.int32, (2 * w, lanes), 0)
    qi = lax.broadcasted_iota(jnp.int32, (2 * w, lanes), 1) % w
    valid = (si > qi) & (si <= qi + w)
    valid_first = valid & ((si >= w) | (step > 0))
    seg = lax.broadcasted_iota(jnp.int32, (1, lanes), 1) // w

    def scores(n):
        kvt_band = band(n)
        out = []
        for kh in range(KV_HEADS):
            q_stack = jnp.concatenate(
                [qt[(kh * GROUP + g) * HEAD:(kh * GROUP + g + 1) * HEAD, n * w:(n + 1) * w]
                 for g in range(GROUP)], axis=1)
            out.append(lax.dot_general(kvt_band[kh * HEAD:(kh + 1) * HEAD], q_stack,
                                       (((0,), (0,)), ((), ())), preferred_element_type=F32))
        return out

    def softmax(n, st):
        mask = valid_first if n == 0 else valid
        out = []
        for kh in range(KV_HEADS):
            sink = sinks_ref[kh * GROUP] * LOG2E
            for g in range(1, GROUP):
                sink = jnp.where(seg == g, sinks_ref[kh * GROUP + g] * LOG2E, sink)
            s = jnp.where(mask, st[kh], MASK_VALUE)
            mx = jnp.maximum(jnp.max(s, axis=0, keepdims=True), sink)
            p = jnp.exp2(s - mx)
            denom = jnp.sum(p, axis=0, keepdims=True) + jnp.exp2(sink - mx)
            out.append((p.astype(BF16), 1.0 / denom))
        return out

    def values(n, pd):
        vt_band = band(n)[kvw:]
        heads = []
        for kh in range(KV_HEADS):
            p, inv_denom = pd[kh]
            ot = jnp.dot(vt_band[kh * HEAD:(kh + 1) * HEAD], p, preferred_element_type=F32) * inv_denom
            heads += [ot[:, g * w:(g + 1) * w] for g in range(GROUP)]
        return jnp.concatenate(heads, axis=0).astype(BF16)

    st = [scores(0)]
    cols = []
    for n in range(ATTN_BLOCKS):
        if n + 1 < ATTN_BLOCKS:
            st.append(scores(n + 1))
        cols.append(values(n, softmax(n, st[n])))
    ot_all = jnp.concatenate(cols, axis=1)
    m = lax.dot_general(ot_all, wo_ref[...], (((0,), (0,)), ((), ())),
                        preferred_element_type=F32) + bo_ref[...]
    o_ref[0] = h + _rms(m, gpost_ref[...])


def _attn_block(h3, kvt3, sinks, g_pre, w_qt, b_qt, w_o, b_o, g_post):
    b, t, d = h3.shape
    kv2 = kvt3.shape[1]
    tq = ATTN_BLOCKS * WINDOW
    tile = pl.BlockSpec((1, tq, d), lambda i, j: (i, j, 0))
    prev = lambda j: jnp.maximum(j * ATTN_BLOCKS - 1, 0)
    params = (g_pre, w_qt, b_qt, w_o, b_o, g_post)
    return pl.pallas_call(
        _attn_kernel,
        out_shape=jax.ShapeDtypeStruct((b, t, d), F32),
        grid=(b, t // tq),
        in_specs=[
            pl.BlockSpec(memory_space=pltpu.SMEM),
            tile,
            pl.BlockSpec((1, kv2, WINDOW), lambda i, j: (i, 0, prev(j))),
            pl.BlockSpec((1, kv2, tq), lambda i, j: (i, 0, j)),
        ] + [_full_spec(p) for p in params],
        out_specs=tile,
        compiler_params=pltpu.CompilerParams(
            dimension_semantics=("parallel", "parallel"), vmem_limit_bytes=VMEM_LIMIT),
        name="swa_block",
    )(sinks, h3, kvt3, kvt3, *params)


def kernel(x, norm_g, ffn_w_in, ffn_w_out, rwkv_mu, rwkv_w_rkv, rwkv_w_o, rwkv_w0, rwkv_w1, rwkv_w2, rwkv_a0, rwkv_a1, rwkv_a2, rwkv_g1, rwkv_g2, rwkv_k_k, rwkv_k_a, rwkv_r_k, rwkv_gn_g, rwkv_gn_b, kv_norm_g, w_kv, b_kv, attn_w_q, attn_b_q, attn_w_o, attn_b_o, attn_sinks):
    b, t, d = x.shape
    m = b * t
    depth = norm_g.shape[0]
    n_a = rwkv_mu.shape[0]
    row = lambda vec: vec.reshape(1, -1).astype(F32)
    col = lambda vec: vec.reshape(-1, 1).astype(F32)
    bf = lambda w: w.astype(BF16)

    lane_head = jnp.arange(GROUP_LANES) // HEAD
    ones_bd = (lane_head[:, None] == lane_head[None, :]).astype(BF16)

    h = x.reshape(m, d)
    kvt_sh = None
    w_in, w_out = ffn_w_in.astype(F32), ffn_w_out.astype(F32)
    kv_params = (row(kv_norm_g), bf(w_kv.T), col(b_kv))
    for layer in range(depth):
        g = norm_g[layer]
        h = _ffn_block(h, row(g[0]), w_in, w_out, row(g[1]), layer, 0)
        mix = None
        if layer < n_a:
            i = layer
            params = (row(g[2]), rwkv_mu[i], bf(rwkv_w_rkv[i]),
                      row(rwkv_w0[i]), bf(rwkv_w1[i]), bf(rwkv_w2[i]),
                      row(rwkv_a0[i]), bf(rwkv_a1[i]), bf(rwkv_a2[i]), bf(rwkv_g1[i]), bf(rwkv_g2[i]),
                      row(rwkv_k_k[i]), row(rwkv_k_a[i]), row(rwkv_r_k[i]), ones_bd)
            r, k, v, lw, kk, a, gate, bonus = _rwkv_prep(h.reshape(b, t, d), params)
            y = _rwkv_scan(r, k, v, lw, kk, a)
            mix = (y.reshape(m, d), bonus.reshape(m, d), gate.reshape(m, d), row(rwkv_gn_g[i]),
                   row(rwkv_gn_b[i]), bf(rwkv_w_o[i]), row(g[3]), ones_bd)
        else:
            j = layer - n_a
            h = _attn_block(h.reshape(b, t, d), kvt_sh, attn_sinks[j].astype(F32), row(g[2]),
                            bf(attn_w_q[j].T), col(attn_b_q[j]), bf(attn_w_o[j]), row(attn_b_o[j]),
                            row(g[3])).reshape(m, d)
        if layer == n_a - 1:
            h, kvt_sh = _ffn_block(h, row(g[4]), w_in, w_out, row(g[5]), layer, 1, mix=mix,
                                   kv=kv_params, seq_len=t)
        else:
            h = _ffn_block(h, row(g[4]), w_in, w_out, row(g[5]), layer, 1, mix=mix)
    return h.reshape(b, t, d)
```
